```python
import math
import jax, jax.numpy as jnp
from jax import lax
import numpy as np

D_MODEL = 2048
BATCH = 2
SEQ = 4096
DEPTH = 4
DEC_BATCH = 32
DEC_SEQ = 1
PAST_LEN = 16384
PAGE_SIZE = 128

N_MIXERS = 3
N_A = (DEPTH + 2) // 3
N_B = (DEPTH + 1) // 3
N_C = DEPTH // 3
HD_A = 128
H_A = D_MODEL // (2 * HD_A)
KV_A = 2
G_A = H_A // KV_A
VD_A = 2 * HD_A
QW_A = H_A * 2 * HD_A
KW_A = KV_A * 2 * HD_A
VW_A = KV_A * VD_A
HD_B = 64
H_B = D_MODEL // HD_B
KV_B = 4
G_B = H_B // KV_B
WINDOW = 128
HD_C = 128
H_C = D_MODEL // HD_C
KV_C = 4
G_C = H_C // KV_C
D_FF = 4 * D_MODEL
ROPE_THETA = 10000.0
EPS = 1e-6
Q_BLOCK = 128

kernel_name = "hybrid_diff_swa_stickbreak_decode_step"


def rms_norm(x, g):
    xf = x.astype(jnp.float32)
    y = xf * lax.rsqrt(jnp.mean(xf * xf, axis=-1, keepdims=True) + EPS)
    return (y * g.astype(jnp.float32)).astype(x.dtype)


def rope(x, pos):
    d = x.shape[-1]
    half = d // 2
    inv_freq = ROPE_THETA ** (-jnp.arange(half, dtype=jnp.float32) * 2.0 / d)
    ang = pos.astype(jnp.float32)[:, None] * inv_freq[None, :]
    cos = jnp.cos(ang)[None, :, None, :]
    sin = jnp.sin(ang)[None, :, None, :]
    xf = x.astype(jnp.float32)
    x1, x2 = xf[..., :half], xf[..., half:]
    return jnp.concatenate([x1 * cos - x2 * sin, x2 * cos + x1 * sin], axis=-1).astype(x.dtype)


def _weighted_values(a, segs, spec):
    outs, start = [], 0
    for k, v, _ in segs:
        n = k.shape[1]
        outs.append(jnp.einsum(spec, a[..., start:start + n].astype(v.dtype), v))
        start += n
    return sum(outs[1:], outs[0])


def diff_lambda(lq1, lk1, lq2, lk2, lam_init):
    f = jnp.float32
    return (jnp.exp(jnp.dot(lq1.astype(f), lk1.astype(f)))
            - jnp.exp(jnp.dot(lq2.astype(f), lk2.astype(f))) + lam_init)


def diff_qkv(h, w, pos):
    B, S, _ = h.shape
    q, k, v = jnp.split(h @ w, [QW_A, QW_A + KW_A], axis=-1)
    q = rope(q.reshape(B, S, H_A * 2, HD_A), pos).reshape(B, S, KV_A, G_A, 2, HD_A)
    k = rope(k.reshape(B, S, KV_A * 2, HD_A), pos).reshape(B, S, KV_A, 2, HD_A)
    v = v.reshape(B, S, KV_A, VD_A)
    return q, k, v


def diff_attn_core(q, segs, lam):
    scale = HD_A ** -0.5
    logits = jnp.concatenate(
        [jnp.where(m[:, None, None, None],
                   jnp.einsum("bqkgcd,bskcd->bkgcqs", q, k).astype(jnp.float32) * scale,
                   -jnp.inf) for k, _, m in segs], axis=-1)
    p = jax.nn.softmax(logits, axis=-1)
    a = p[:, :, :, 0] - lam * p[:, :, :, 1]
    return _weighted_values(a, segs, "bkgqs,bskd->bqkgd")


def diff_out(o, g_sub, lam_init, w_out):
    B, Q = o.shape[:2]
    o = rms_norm(o, g_sub) * (1.0 - lam_init)
    return o.reshape(B, Q, H_A * VD_A) @ w_out


def diff_attn_prompt(h, w_qkv, lam, lam_init, g_sub, w_out, pos):
    q, k, v = diff_qkv(h, w_qkv, pos)
    B, S = h.shape[:2]
    kpos = jnp.arange(S)

    def block(i):
        qb = lax.dynamic_slice_in_dim(q, i * Q_BLOCK, Q_BLOCK, axis=1)
        qpos = i * Q_BLOCK + jnp.arange(Q_BLOCK)
        mask = (kpos[None, :] <= qpos[:, None])[None]
        return diff_attn_core(qb, [(k, v, mask)], lam)

    o = lax.map(block, jnp.arange(S // Q_BLOCK))
    o = jnp.moveaxis(o, 0, 1).reshape(B, S, KV_A, G_A, VD_A)
    return diff_out(o, g_sub, lam_init, w_out), k, v


def diff_attn_sample(h, cache_k, cache_v, j, page_table, w_qkv, lam, lam_init, g_sub, w_out, pos):
    q, k, v = diff_qkv(h, w_qkv, pos)
    B, Q = h.shape[:2]
    past = page_table.shape[1] * PAGE_SIZE
    k_past = cache_k[j, page_table].reshape(B, past, KV_A, 2, HD_A)
    v_past = cache_v[j, page_table].reshape(B, past, KV_A, VD_A)
    m_past = jnp.ones((1, Q, past), dtype=bool)
    m_new = jnp.tril(jnp.ones((Q, Q), dtype=bool))[None]
    o = diff_attn_core(q, [(k_past, v_past, m_past), (k, v, m_new)], lam)
    return diff_out(o, g_sub, lam_init, w_out), k, v


def swa_qkv(h, w, pos):
    B, S, _ = h.shape
    q, k, v = jnp.split(h @ w, [H_B * HD_B, H_B * HD_B + KV_B * HD_B], axis=-1)
    q = rope(q.reshape(B, S, H_B, HD_B), pos).reshape(B, S, KV_B, G_B, HD_B)
    k = rope(k.reshape(B, S, KV_B, HD_B), pos)
    v = v.reshape(B, S, KV_B, HD_B)
    return q, k, v


def sink_attn_core(q, segs, sink):
    scale = HD_B ** -0.5
    logits = jnp.concatenate(
        [jnp.where(m[:, None, None],
                   jnp.einsum("bqkgd,bskd->bkgqs", q, k).astype(jnp.float32) * scale,
                   -jnp.inf) for k, _, m in segs], axis=-1)
    sinkb = sink.astype(jnp.float32)[None, :, :, None, None]
    mx = jnp.maximum(jnp.max(logits, axis=-1, keepdims=True), sinkb)
    e = jnp.exp(logits - mx)
    p = e / (jnp.sum(e, axis=-1, keepdims=True) + jnp.exp(sinkb - mx))
    return _weighted_values(p, segs, "bkgqs,bskd->bqkgd")


def _band(qpos, kpos):
    d = qpos[..., :, None] - kpos[..., None, :]
    return (d >= 0) & (d <= WINDOW) & (kpos[..., None, :] >= 0)


def swa_prompt(h, w_qkv, sink, w_out, pos):
    q, k, v = swa_qkv(h, w_qkv, pos)
    B, S = h.shape[:2]
    W = WINDOW
    nb = S // W
    qb = q.reshape(B * nb, W, KV_B, G_B, HD_B)
    kb = k.reshape(B, nb, W, KV_B, HD_B)
    vb = v.reshape(B, nb, W, KV_B, HD_B)
    k_prev = jnp.concatenate([jnp.zeros_like(kb[:, :1]), kb[:, :-1]], axis=1).reshape(B * nb, W, KV_B, HD_B)
    v_prev = jnp.concatenate([jnp.zeros_like(vb[:, :1]), vb[:, :-1]], axis=1).reshape(B * nb, W, KV_B, HD_B)
    qpos = jnp.arange(nb)[:, None] * W + jnp.arange(W)[None, :]
    m_prev = jnp.tile(_band(qpos, qpos - W), (B, 1, 1))
    m_cur = jnp.tile(_band(qpos, qpos), (B, 1, 1))
    o = sink_attn_core(qb, [(k_prev, v_prev, m_prev),
                            (kb.reshape(B * nb, W, KV_B, HD_B), vb.reshape(B * nb, W, KV_B, HD_B), m_cur)], sink)
    y = o.reshape(B, S, H_B * HD_B) @ w_out
    return y, k[:, -W:], v[:, -W:]


def swa_sample(h, buf_k, buf_v, w_qkv, sink, w_out, pos):
    q, k, v = swa_qkv(h, w_qkv, pos)
    B, Q = h.shape[:2]
    wb = buf_k.shape[1]
    buf_pos = PAST_LEN - wb + jnp.arange(wb)
    m_buf = _band(pos, buf_pos)[None]
    m_new = _band(pos, pos)[None]
    o = sink_attn_core(q, [(buf_k, buf_v, m_buf), (k, v, m_new)], sink)
    y = o.reshape(B, Q, H_B * HD_B) @ w_out
    new_k = jnp.concatenate([buf_k, k], axis=1)[:, -wb:]
    new_v = jnp.concatenate([buf_v, v], axis=1)[:, -wb:]
    return y, new_k, new_v


def sb_qkv(h, w):
    B, S, _ = h.shape
    q, k, v = jnp.split(h @ w, [H_C * HD_C, H_C * HD_C + KV_C * HD_C], axis=-1)
    return (q.reshape(B, S, KV_C, G_C, HD_C), k.reshape(B, S, KV_C, HD_C), v.reshape(B, S, KV_C, HD_C))


def stick_breaking_core(q, segs):
    scale = HD_C ** -0.5
    z = jnp.concatenate([jnp.einsum("bqkgd,bskd->bkgqs", q, k).astype(jnp.float32) * scale
                         for k, _, _ in segs], axis=-1)
    mask = jnp.concatenate([m for _, _, m in segs], axis=-1)[:, None, None]
    log_beta = jax.nn.log_sigmoid(z)
    log_keep = jnp.where(mask, jax.nn.log_sigmoid(-z), 0.0)
    later = lax.cumsum(log_keep, axis=z.ndim - 1, reverse=True) - log_keep
    a = jnp.where(mask, jnp.exp(log_beta + later), 0.0)
    return _weighted_values(a, segs, "bkgqs,bskd->bqkgd")


def sb_prompt(h, w_qkv, w_out):
    q, k, v = sb_qkv(h, w_qkv)
    B, S = h.shape[:2]
    kpos = jnp.arange(S)

    def block(i):
        qb = lax.dynamic_slice_in_dim(q, i * Q_BLOCK, Q_BLOCK, axis=1)
        qpos = i * Q_BLOCK + jnp.arange(Q_BLOCK)
        mask = (kpos[None, :] < qpos[:, None])[None]
        return stick_breaking_core(qb, [(k, v, mask)])

    o = lax.map(block, jnp.arange(S // Q_BLOCK))
    o = jnp.moveaxis(o, 0, 1).reshape(B, S, H_C * HD_C)
    return o @ w_out, k, v


def sb_sample(h, cache_k, cache_v, j, page_table, w_qkv, w_out):
    q, k, v = sb_qkv(h, w_qkv)
    B, Q = h.shape[:2]
    past = page_table.shape[1] * PAGE_SIZE
    k_past = cache_k[j, page_table].reshape(B, past, KV_C, HD_C)
    v_past = cache_v[j, page_table].reshape(B, past, KV_C, HD_C)
    m_past = jnp.ones((1, Q, past), dtype=bool)
    m_new = jnp.tril(jnp.ones((Q, Q), dtype=bool), k=-1)[None]
    o = stick_breaking_core(q, [(k_past, v_past, m_past), (k, v, m_new)])
    return o.reshape(B, Q, H_C * HD_C) @ w_out, k, v


def sq_relu_mlp(h, w_up, w_down):
    return jnp.square(jax.nn.relu(h @ w_up)) @ w_down


def setup_inputs(seed: int = 0) -> dict:
    key = jax.random.key(seed)
    ks = jax.random.split(key, 32)
    f32 = jnp.float32
    n_pages = PAST_LEN // PAGE_SIZE
    n_used = DEC_BATCH * n_pages
    n_pool = n_used + n_used // 4
    win = min(WINDOW, PAST_LEN)

    def nrm(k, shape, s):
        return jax.random.normal(k, shape, f32) * s

    def gain(k, shape):
        return 1.0 + 0.01 * jax.random.normal(k, shape, f32)

    page_table = jax.random.permutation(ks[0], n_pool)[:n_used].reshape(DEC_BATCH, n_pages).astype(jnp.int32)
    fan = D_MODEL ** -0.5
    return {
        "x_prompt": nrm(ks[1], (BATCH, SEQ, D_MODEL), 1.0),
        "x_sample": nrm(ks[2], (DEC_BATCH, DEC_SEQ, D_MODEL), 1.0),
        "cache_a_k": nrm(ks[3], (N_A, n_pool, PAGE_SIZE, KV_A, 2, HD_A), 1.0),
        "cache_a_v": nrm(ks[4], (N_A, n_pool, PAGE_SIZE, KV_A, VD_A), 1.0),
        "cache_c_k": nrm(ks[5], (N_C, n_pool, PAGE_SIZE, KV_C, HD_C), 1.0),
        "cache_c_v": nrm(ks[6], (N_C, n_pool, PAGE_SIZE, KV_C, HD_C), 1.0),
        "state_win_k": nrm(ks[7], (N_B, DEC_BATCH, win, KV_B, HD_B), 1.0),
        "state_win_v": nrm(ks[8], (N_B, DEC_BATCH, win, KV_B, HD_B), 1.0),
        "page_table": page_table,
        "g_mix_pre": gain(ks[9], (DEPTH, D_MODEL)),
        "g_mix_post": gain(ks[10], (DEPTH, D_MODEL)),
        "g_ffn_pre": gain(ks[11], (DEPTH, D_MODEL)),
        "g_ffn_post": gain(ks[12], (DEPTH, D_MODEL)),
        "w_a_qkv": nrm(ks[13], (N_A, D_MODEL, QW_A + KW_A + VW_A), fan),
        "lam_q1": nrm(ks[14], (N_A, HD_A), 0.1),
        "lam_k1": nrm(ks[15], (N_A, HD_A), 0.1),
        "lam_q2": nrm(ks[16], (N_A, HD_A), 0.1),
        "lam_k2": nrm(ks[17], (N_A, HD_A), 0.1),
        "g_a_sub": gain(ks[18], (N_A, VD_A)),
        "w_a_out": nrm(ks[19], (N_A, H_A * VD_A, D_MODEL), (H_A * VD_A) ** -0.5),
        "w_b_qkv": nrm(ks[20], (N_B, D_MODEL, H_B * HD_B + 2 * KV_B * HD_B), fan),
        "b_sink": nrm(ks[21], (N_B, H_B), 0.5),
        "w_b_out": nrm(ks[22], (N_B, H_B * HD_B, D_MODEL), (H_B * HD_B) ** -0.5),
        "w_c_qkv": nrm(ks[23], (N_C, D_MODEL, H_C * HD_C + 2 * KV_C * HD_C), fan),
        "w_c_out": nrm(ks[24], (N_C, H_C * HD_C, D_MODEL), (H_C * HD_C) ** -0.5),
        "w_up": nrm(ks[25], (DEPTH, D_MODEL, D_FF), fan),
        "w_down": nrm(ks[26], (DEPTH, D_FF, D_MODEL), D_FF ** -0.5),
    }


def reference(x_prompt, x_sample, cache_a_k, cache_a_v, cache_c_k, cache_c_v,
              state_win_k, state_win_v, page_table,
              g_mix_pre, g_mix_post, g_ffn_pre, g_ffn_post,
              w_a_qkv, lam_q1, lam_k1, lam_q2, lam_k2, g_a_sub, w_a_out,
              w_b_qkv, b_sink, w_b_out, w_c_qkv, w_c_out, w_up, w_down):
    pos_p = jnp.arange(SEQ, dtype=jnp.int32)
    pos_s = PAST_LEN + jnp.arange(DEC_SEQ, dtype=jnp.int32)
    xp, xs = x_prompt, x_sample
    a_kp, a_vp, a_ks, a_vs = [], [], [], []
    b_kp, b_vp, b_ks, b_vs = [], [], [], []
    c_kp, c_vp, c_ks, c_vs = [], [], [], []
    for i in range(DEPTH):
        kind, j = i % N_MIXERS, i // N_MIXERS
        hp = rms_norm(xp, g_mix_pre[i])
        hs = rms_norm(xs, g_mix_pre[i])
        if kind == 0:
            lam_init = 0.8 - 0.6 * math.exp(-0.3 * i)
            lam = diff_lambda(lam_q1[j], lam_k1[j], lam_q2[j], lam_k2[j], lam_init)
            mp, kp, vp = diff_attn_prompt(hp, w_a_qkv[j], lam, lam_init, g_a_sub[j], w_a_out[j], pos_p)
            ms, ksn, vsn = diff_attn_sample(hs, cache_a_k, cache_a_v, j, page_table, w_a_qkv[j], lam,
                                            lam_init, g_a_sub[j], w_a_out[j], pos_s)
            a_kp.append(kp); a_vp.append(vp); a_ks.append(ksn); a_vs.append(vsn)
        elif kind == 1:
            sink = b_sink[j].reshape(KV_B, G_B)
            mp, kp, vp = swa_prompt(hp, w_b_qkv[j], sink, w_b_out[j], pos_p)
            ms, ksn, vsn = swa_sample(hs, state_win_k[j], state_win_v[j], w_b_qkv[j], sink, w_b_out[j], pos_s)
            b_kp.append(kp); b_vp.append(vp); b_ks.append(ksn); b_vs.append(vsn)
        else:
            mp, kp, vp = sb_prompt(hp, w_c_qkv[j], w_c_out[j])
            ms, ksn, vsn = sb_sample(hs, cache_c_k, cache_c_v, j, page_table, w_c_qkv[j], w_c_out[j])
            c_kp.append(kp); c_vp.append(vp); c_ks.append(ksn); c_vs.append(vsn)
        xp = xp + rms_norm(mp, g_mix_post[i])
        xs = xs + rms_norm(ms, g_mix_post[i])
        xp = xp + rms_norm(sq_relu_mlp(rms_norm(xp, g_ffn_pre[i]), w_up[i], w_down[i]), g_ffn_post[i])
        xs = xs + rms_norm(sq_relu_mlp(rms_norm(xs, g_ffn_pre[i]), w_up[i], w_down[i]), g_ffn_post[i])
    new_state = [jnp.stack(rows) for rows in (a_kp, a_vp, a_ks, a_vs, b_kp, b_vp, b_ks, b_vs,
                                              c_kp, c_vp, c_ks, c_vs)]
    return (xp, xs, *new_state)
```

```python
import functools
import math

import numpy as np
import jax
import jax.numpy as jnp
from jax import lax
from jax.experimental import pallas as pl
from jax.experimental.pallas import tpu as pltpu

D_MODEL = 2048
N_MIXERS = 3
HD_A, KV_A = 128, 2
H_A = D_MODEL // (2 * HD_A)
G_A = H_A // KV_A
VD_A = 2 * HD_A
HD_B, KV_B = 64, 4
H_B = D_MODEL // HD_B
G_B = H_B // KV_B
WINDOW = 128
HD_C, KV_C = 128, 4
H_C = D_MODEL // HD_C
G_C = H_C // KV_C
ROPE_THETA = 10000.0
EPS = 1e-6

LANES = 128
VMEM_LIMIT = 56 * 1024 * 1024
F32 = jnp.float32
BF16 = jnp.bfloat16
NEG_INF = float("-inf")


def _params(sem):
    return pltpu.CompilerParams(dimension_semantics=sem, vmem_limit_bytes=VMEM_LIMIT)


def _rms(x, g):
    return x * lax.rsqrt(jnp.mean(x * x, axis=-1, keepdims=True) + EPS) * g


def _dot(a, b):
    return jnp.dot(a, b, preferred_element_type=F32)


def _dot_nt(a, b):
    return lax.dot_general(a, b, (((1,), (1,)), ((), ())), preferred_element_type=F32)


def _log_sigmoid(z):
    return jnp.minimum(z, 0.0) - jnp.log1p(jnp.exp(-jnp.abs(z)))


def _qkv_kernel(*refs, rope_kind, kw, vw):
    if rope_kind is None:
        x_ref, g_ref, w_ref, q_ref, k32_ref, v32_ref, k16_ref, v16_ref = refs
    else:
        x_ref, g_ref, w_ref, tab_ref, q_ref, k32_ref, v32_ref, k16_ref, v16_ref = refs
    xn = _rms(x_ref[...], g_ref[...]).astype(BF16)

    def rope(y):
        if rope_kind is None:
            return y
        if rope_kind == 128:
            return y * tab_ref[0] + pltpu.roll(y, 64, 1) * tab_ref[1]
        return (y * tab_ref[0] + pltpu.roll(y, 96, 1) * tab_ref[1]
                + pltpu.roll(y, 32, 1) * tab_ref[2])

    def project(c0, width, store):
        step = min(width, 4 * LANES)
        for s0 in range(0, width, step):
            y = _dot(xn, w_ref[:, c0 + s0:c0 + s0 + step])
            for t in range(step // LANES):
                store(s0 + t * LANES, y[:, t * LANES:(t + 1) * LANES])

    def store_q(c, y):
        q_ref[:, c:c + LANES] = rope(y).astype(BF16)

    def store_k(c, y):
        y = rope(y)
        k32_ref[:, c:c + LANES] = y
        k16_ref[:, c:c + LANES] = y.astype(BF16)

    def store_v(c, y):
        v32_ref[:, c:c + LANES] = y
        v16_ref[:, c:c + LANES] = y.astype(BF16)

    project(0, D_MODEL, store_q)
    project(D_MODEL, kw, store_k)
    project(D_MODEL + kw, vw, store_v)


def _qkv(x, g, w, tab, *, rope_kind, kw, vw, tm):
    m = x.shape[0]
    n = w.shape[1]
    grid = (m // tm,)
    in_specs = [pl.BlockSpec((tm, D_MODEL), lambda i: (i, 0)),
                pl.BlockSpec((1, D_MODEL), lambda i: (0, 0)),
                pl.BlockSpec((D_MODEL, n), lambda i: (0, 0))]
    args = [x, g.reshape(1, D_MODEL), w]
    if rope_kind is not None:
        nt = tab.shape[1] // tm
        in_specs.append(pl.BlockSpec((3, tm, LANES), lambda i: (0, i % nt, 0)))
        args.append(tab)
    out_shape = (jax.ShapeDtypeStruct((m, D_MODEL), BF16),
                 jax.ShapeDtypeStruct((m, kw), F32), jax.ShapeDtypeStruct((m, vw), F32),
                 jax.ShapeDtypeStruct((m, kw), BF16), jax.ShapeDtypeStruct((m, vw), BF16))
    out_specs = (pl.BlockSpec((tm, D_MODEL), lambda i: (i, 0)),
                 pl.BlockSpec((tm, kw), lambda i: (i, 0)), pl.BlockSpec((tm, vw), lambda i: (i, 0)),
                 pl.BlockSpec((tm, kw), lambda i: (i, 0)), pl.BlockSpec((tm, vw), lambda i: (i, 0)))
    return pl.pallas_call(
        functools.partial(_qkv_kernel, rope_kind=rope_kind, kw=kw, vw=vw),
        grid=grid, in_specs=in_specs, out_specs=out_specs, out_shape=out_shape,
        compiler_params=_params(("parallel",)), name="qkv_proj")(*args)


def _rope_tables(pos, hd):
    half = hd // 2
    inv_freq = ROPE_THETA ** (-jnp.arange(half, dtype=F32) * 2.0 / hd)
    ang = pos.astype(F32)[:, None] * inv_freq[None, :]
    cos, sin = jnp.cos(ang), jnp.sin(ang)
    zero = jnp.zeros_like(sin)
    reps = LANES // hd
    cos_t = jnp.tile(jnp.concatenate([cos, cos], axis=-1), (1, reps))
    if hd == LANES:
        s1 = jnp.concatenate([-sin, sin], axis=-1)
        s2 = jnp.zeros_like(s1)
    else:
        s1 = jnp.tile(jnp.concatenate([-sin, zero], axis=-1), (1, reps))
        s2 = jnp.tile(jnp.concatenate([zero, sin], axis=-1), (1, reps))
    return jnp.stack([cos_t, s1, s2])


def _outproj_kernel(o_ref, w_ref, x_ref, g_ref, y_ref):
    y = _dot(o_ref[...].astype(BF16), w_ref[...])
    y_ref[...] = x_ref[...] + _rms(y, g_ref[...])


def _outproj(o, w, x, g, *, tm):
    m = x.shape[0]
    return pl.pallas_call(
        _outproj_kernel, grid=(m // tm,),
        in_specs=[pl.BlockSpec((tm, D_MODEL), lambda i: (i, 0)),
                  pl.BlockSpec((D_MODEL, D_MODEL), lambda i: (0, 0)),
                  pl.BlockSpec((tm, D_MODEL), lambda i: (i, 0)),
                  pl.BlockSpec((1, D_MODEL), lambda i: (0, 0))],
        out_specs=pl.BlockSpec((tm, D_MODEL), lambda i: (i, 0)),
        out_shape=jax.ShapeDtypeStruct((m, D_MODEL), F32),
        compiler_params=_params(("parallel",)), name="out_proj")(o, w, x, g.reshape(1, D_MODEL))


def _mlp_kernel(x_ref, gpre_ref, wup_ref, wdn_ref, gpost_ref, y_ref, xn_sc, acc_sc):
    f = pl.program_id(1)

    @pl.when(f == 0)
    def _():
        xn_sc[...] = _rms(x_ref[...], gpre_ref[...]).astype(BF16)
        acc_sc[...] = jnp.zeros_like(acc_sc)

    h = jnp.maximum(_dot(xn_sc[...], wup_ref[...]), 0.0)
    acc_sc[...] += _dot((h * h).astype(BF16), wdn_ref[...])

    @pl.when(f == pl.num_programs(1) - 1)
    def _():
        y_ref[...] = x_ref[...] + _rms(acc_sc[...], gpost_ref[...])


def _mlp(x, gpre, wup, wdn, gpost, *, tm, tf):
    m = x.shape[0]
    d_ff = wup.shape[1]
    return pl.pallas_call(
        _mlp_kernel, grid=(m // tm, d_ff // tf),
        in_specs=[pl.BlockSpec((tm, D_MODEL), lambda i, f: (i, 0)),
                  pl.BlockSpec((1, D_MODEL), lambda i, f: (0, 0)),
                  pl.BlockSpec((D_MODEL, tf), lambda i, f: (0, f)),
                  pl.BlockSpec((tf, D_MODEL), lambda i, f: (f, 0)),
                  pl.BlockSpec((1, D_MODEL), lambda i, f: (0, 0))],
        out_specs=pl.BlockSpec((tm, D_MODEL), lambda i, f: (i, 0)),
        out_shape=jax.ShapeDtypeStruct((m, D_MODEL), F32),
        scratch_shapes=[pltpu.VMEM((tm, D_MODEL), BF16), pltpu.VMEM((tm, D_MODEL), F32)],
        compiler_params=_params(("parallel", "arbitrary")), name="mlp")(
            x, gpre.reshape(1, D_MODEL), wup, wdn, gpost.reshape(1, D_MODEL))


def _diff_lambda(lam_ref, lam_init):
    d1 = jnp.sum(lam_ref[0:1, :] * lam_ref[1:2, :], axis=-1, keepdims=True)
    d2 = jnp.sum(lam_ref[2:3, :] * lam_ref[3:4, :], axis=-1, keepdims=True)
    return jnp.exp(d1) - jnp.exp(d2) + lam_init


def _attn_a_kernel(lam_ref, gsub_ref, q_ref, k_ref, v_ref, o_ref, m_sc, l_sc, acc_sc, *, lam_init):
    qi = pl.program_id(2)
    ki = pl.program_id(3)
    scale = HD_A ** -0.5

    @pl.when(ki == 0)
    def _():
        m_sc[...] = jnp.full_like(m_sc, NEG_INF)
        l_sc[...] = jnp.zeros_like(l_sc)
        acc_sc[...] = jnp.zeros_like(acc_sc)

    def step(diag):
        k = k_ref[...]
        v = v_ref[...]
        tq, tk = q_ref.shape[0], k.shape[0]
        if diag:
            keep = (lax.broadcasted_iota(jnp.int32, (tq, tk), 0)
                    >= lax.broadcasted_iota(jnp.int32, (tq, tk), 1))
        for g in range(G_A):
            for c in range(2):
                idx = g * 2 + c
                q = q_ref[:, idx * HD_A:(idx + 1) * HD_A]
                s = _dot_nt(q, k[:, c * HD_A:(c + 1) * HD_A]) * scale
                if diag:
                    s = jnp.where(keep, s, NEG_INF)
                m_prev = m_sc[idx]
                m_new = jnp.maximum(m_prev, jnp.max(s, axis=-1, keepdims=True))
                alpha = jnp.exp(m_prev - m_new)
                p = jnp.exp(s - m_new)
                l_sc[idx] = alpha * l_sc[idx] + jnp.sum(p, axis=-1, keepdims=True)
                acc_sc[idx] = alpha * acc_sc[idx] + _dot(p.astype(BF16), v)
                m_sc[idx] = m_new

    @pl.when(ki < qi)
    def _():
        step(False)

    @pl.when(ki == qi)
    def _():
        step(True)
        lam = _diff_lambda(lam_ref, lam_init)
        for g in range(G_A):
            o = acc_sc[2 * g] / l_sc[2 * g] - lam * (acc_sc[2 * g + 1] / l_sc[2 * g + 1])
            o = _rms(o, gsub_ref[...]) * (1.0 - lam_init)
            o_ref[:, g * VD_A:(g + 1) * VD_A] = o.astype(o_ref.dtype)


def _attn_a_prompt(q, k, v, lamvec, gsub, *, batch, seq, lam_init, tq):
    nq = seq // tq
    gw = G_A * VD_A
    return pl.pallas_call(
        functools.partial(_attn_a_kernel, lam_init=lam_init),
        grid=(batch, KV_A, nq, nq),
        in_specs=[pl.BlockSpec((4, HD_A), lambda b, h, i, j: (0, 0)),
                  pl.BlockSpec((1, VD_A), lambda b, h, i, j: (0, 0)),
                  pl.BlockSpec((tq, gw), lambda b, h, i, j: (b * nq + i, h)),
                  pl.BlockSpec((tq, 2 * HD_A), lambda b, h, i, j: (b * nq + jnp.minimum(i, j), h)),
                  pl.BlockSpec((tq, VD_A), lambda b, h, i, j: (b * nq + jnp.minimum(i, j), h))],
        out_specs=pl.BlockSpec((tq, gw), lambda b, h, i, j: (b * nq + i, h)),
        out_shape=jax.ShapeDtypeStruct((batch * seq, H_A * VD_A), BF16),
        scratch_shapes=[pltpu.VMEM((2 * G_A, tq, 1), F32), pltpu.VMEM((2 * G_A, tq, 1), F32),
                        pltpu.VMEM((2 * G_A, tq, VD_A), F32)],
        compiler_params=_params(("parallel", "parallel", "parallel", "arbitrary")),
        name="diff_attn_prompt")(lamvec, gsub.reshape(1, VD_A), q, k, v)


def _attn_a_sample_kernel(pt_ref, lam_ref, gsub_ref, qz_ref, kn_ref, vn_ref, *rest, pps, lam_init):
    k_refs, v_refs = rest[:pps], rest[pps:2 * pps]
    o_ref, m_sc, l_sc, acc_sc = rest[2 * pps:]
    st = pl.program_id(1)
    scale = HD_A ** -0.5

    @pl.when(st == 0)
    def _():
        for kv in range(KV_A):
            kn = kn_ref[0, :, kv * 2 * HD_A:(kv + 1) * 2 * HD_A].astype(BF16).astype(F32)
            vn = vn_ref[0, :, kv * VD_A:(kv + 1) * VD_A].astype(BF16).astype(F32)
            s = jnp.sum(qz_ref[0, kv].astype(F32) * kn, axis=-1, keepdims=True) * scale
            m_sc[kv] = s
            l_sc[kv] = jnp.ones_like(s)
            acc_sc[kv] = jnp.broadcast_to(vn, acc_sc.shape[1:])

    for kv in range(KV_A):
        qz = qz_ref[0, kv]
        ss = []
        for i in range(pps):
            kp = jnp.concatenate([k_refs[i][:, kv, 0, :], k_refs[i][:, kv, 1, :]], axis=1)
            ss.append(_dot_nt(qz, kp.astype(BF16)) * scale)
        m_prev = m_sc[kv]
        m_new = m_prev
        for s in ss:
            m_new = jnp.maximum(m_new, jnp.max(s, axis=-1, keepdims=True))
        alpha = jnp.exp(m_prev - m_new)
        l = alpha * l_sc[kv]
        acc = alpha * acc_sc[kv]
        for i in range(pps):
            p = jnp.exp(ss[i] - m_new)
            l = l + jnp.sum(p, axis=-1, keepdims=True)
            acc = acc + _dot(p.astype(BF16), v_refs[i][:, kv, :].astype(BF16))
        m_sc[kv] = m_new
        l_sc[kv] = l
        acc_sc[kv] = acc

    @pl.when(st == pl.num_programs(1) - 1)
    def _():
        lam = _diff_lambda(lam_ref, lam_init)
        for kv in range(KV_A):
            on = acc_sc[kv] / l_sc[kv]
            o = on[0:G_A] - lam * on[G_A:2 * G_A]
            o_ref[0, kv] = _rms(o, gsub_ref[...]) * (1.0 - lam_init)


def _attn_a_sample(pt, lamvec, gsub, qz, kn, vn, cache_k, cache_v, *, layer, lam_init, pps):
    db = qz.shape[0]
    n_pages = pt.shape[0] // db
    page = cache_k.shape[2]

    def page_map(i, nd):
        return lambda b, s, pt_ref: (layer, pt_ref[b * n_pages + s * pps + i]) + (0,) * nd

    in_specs = [pl.BlockSpec((4, HD_A), lambda b, s, pt_ref: (0, 0)),
                pl.BlockSpec((1, VD_A), lambda b, s, pt_ref: (0, 0)),
                pl.BlockSpec((1, KV_A, 2 * G_A, 2 * HD_A), lambda b, s, pt_ref: (b, 0, 0, 0)),
                pl.BlockSpec((1, 1, KV_A * 2 * HD_A), lambda b, s, pt_ref: (b, 0, 0)),
                pl.BlockSpec((1, 1, KV_A * VD_A), lambda b, s, pt_ref: (b, 0, 0))]
    in_specs += [pl.BlockSpec((None, None, page, KV_A, 2, HD_A), page_map(i, 4)) for i in range(pps)]
    in_specs += [pl.BlockSpec((None, None, page, KV_A, VD_A), page_map(i, 3)) for i in range(pps)]
    gs = pltpu.PrefetchScalarGridSpec(
        num_scalar_prefetch=1, grid=(db, n_pages // pps), in_specs=in_specs,
        out_specs=pl.BlockSpec((1, KV_A, G_A, VD_A), lambda b, s, pt_ref: (b, 0, 0, 0)),
        scratch_shapes=[pltpu.VMEM((KV_A, 2 * G_A, 1), F32), pltpu.VMEM((KV_A, 2 * G_A, 1), F32),
                        pltpu.VMEM((KV_A, 2 * G_A, VD_A), F32)])
    return pl.pallas_call(
        functools.partial(_attn_a_sample_kernel, pps=pps, lam_init=lam_init),
        grid_spec=gs, out_shape=jax.ShapeDtypeStruct((db, KV_A, G_A, VD_A), F32),
        compiler_params=_params(("parallel", "arbitrary")), name="diff_attn_sample")(
            pt, lamvec, gsub.reshape(1, VD_A), qz, kn, vn,
            *([cache_k] * pps), *([cache_v] * pps))


def _attn_b_kernel(sink_ref, q_ref, kc_ref, kp_ref, vc_ref, vp_ref, o_ref):
    i = pl.program_id(1)
    w = q_ref.shape[0]
    scale = HD_B ** -0.5
    kk = jnp.concatenate([kp_ref[...], kc_ref[...]], axis=0)
    vv = jnp.concatenate([vp_ref[...], vc_ref[...]], axis=0)
    row = lax.broadcasted_iota(jnp.int32, (w, 2 * w), 0)
    col = lax.broadcasted_iota(jnp.int32, (w, 2 * w), 1)
    keep = (col >= row) & (col <= row + WINDOW) & ((i > 0) | (col >= w))
    for kv in range(KV_B):
        k = kk[:, kv * HD_B:(kv + 1) * HD_B]
        v = vv[:, kv * HD_B:(kv + 1) * HD_B]
        outs = []
        for g in range(G_B):
            h = kv * G_B + g
            q = q_ref[:, h * HD_B:(h + 1) * HD_B]
            s = jnp.where(keep, _dot_nt(q, k) * scale, NEG_INF)
            sink = sink_ref[h]
            mx = jnp.maximum(jnp.max(s, axis=-1, keepdims=True), sink)
            e = jnp.exp(s - mx)
            denom = jnp.sum(e, axis=-1, keepdims=True) + jnp.exp(sink - mx)
            outs.append(_dot(e.astype(BF16), v) / denom)
        for t in range(G_B // 2):
            h0 = kv * G_B + 2 * t
            o_ref[:, h0 * HD_B:(h0 + 2) * HD_B] = jnp.concatenate(
                [outs[2 * t], outs[2 * t + 1]], axis=1).astype(o_ref.dtype)


def _attn_b_prompt(sink, q, k, v, *, batch, seq):
    nb = seq // WINDOW
    kvw = KV_B * HD_B
    cur = lambda b, i: (b * nb + i, 0)
    prev = lambda b, i: (b * nb + jnp.maximum(i - 1, 0), 0)
    return pl.pallas_call(
        _attn_b_kernel, grid=(batch, nb),
        in_specs=[pl.BlockSpec(memory_space=pltpu.SMEM),
                  pl.BlockSpec((WINDOW, D_MODEL), cur),
                  pl.BlockSpec((WINDOW, kvw), cur), pl.BlockSpec((WINDOW, kvw), prev),
                  pl.BlockSpec((WINDOW, kvw), cur), pl.BlockSpec((WINDOW, kvw), prev)],
        out_specs=pl.BlockSpec((WINDOW, D_MODEL), cur),
        out_shape=jax.ShapeDtypeStruct((batch * seq, D_MODEL), BF16),
        compiler_params=_params(("parallel", "parallel")), name="swa_prompt")(sink, q, k, k, v, v)


def _attn_b_sample_kernel(sink_ref, qz_ref, kn_ref, vn_ref, bk_ref, bv_ref, o_ref, nk_ref, nv_ref,
                          *, past_len):
    wb = bk_ref.shape[1]
    scale = HD_B ** -0.5
    bk = bk_ref[0]
    bv = bv_ref[0]
    kn = kn_ref[0]
    vn = vn_ref[0]
    nk_ref[0, 0:wb - 1, :] = bk[1:wb, :]
    nk_ref[0, wb - 1:wb, :] = kn
    nv_ref[0, 0:wb - 1, :] = bv[1:wb, :]
    nv_ref[0, wb - 1:wb, :] = vn
    dist = wb - lax.broadcasted_iota(jnp.int32, (G_B, wb), 1)
    keep = (dist >= 0) & (dist <= WINDOW) & (past_len - dist >= 0)
    bk16 = bk.astype(BF16)
    bv16 = bv.astype(BF16)
    kn16 = kn.astype(BF16).astype(F32)
    vn16 = vn.astype(BF16).astype(F32)
    for kv in range(KV_B):
        qz = qz_ref[0, kv]
        s = jnp.where(keep, _dot_nt(qz, bk16) * scale, NEG_INF)
        s_new = jnp.sum(qz.astype(F32) * kn16, axis=-1, keepdims=True) * scale
        sink = sink_ref[kv]
        mx = jnp.maximum(jnp.maximum(jnp.max(s, axis=-1, keepdims=True), s_new), sink)
        e = jnp.exp(s - mx)
        e_new = jnp.exp(s_new - mx)
        denom = jnp.sum(e, axis=-1, keepdims=True) + e_new + jnp.exp(sink - mx)
        o = _dot(e.astype(BF16), bv16) + e_new.astype(BF16).astype(F32) * vn16
        o_ref[0, kv] = (o / denom)[:, kv * HD_B:(kv + 1) * HD_B]


def _attn_b_sample(sink, qz, kn, vn, buf_k, buf_v, *, past_len):
    db, wb, kvw = buf_k.shape
    row = lambda b: (b, 0, 0)
    return pl.pallas_call(
        functools.partial(_attn_b_sample_kernel, past_len=past_len), grid=(db,),
        in_specs=[pl.BlockSpec((KV_B, G_B, 1), lambda b: (0, 0, 0)),
                  pl.BlockSpec((1, KV_B, G_B, kvw), lambda b: (b, 0, 0, 0)),
                  pl.BlockSpec((1, 1, kvw), row), pl.BlockSpec((1, 1, kvw), row),
                  pl.BlockSpec((1, wb, kvw), row), pl.BlockSpec((1, wb, kvw), row)],
        out_specs=(pl.BlockSpec((1, KV_B, G_B, HD_B), lambda b: (b, 0, 0, 0)),
                   pl.BlockSpec((1, wb, kvw), row), pl.BlockSpec((1, wb, kvw), row)),
        out_shape=(jax.ShapeDtypeStruct((db, KV_B, G_B, HD_B), F32),
                   jax.ShapeDtypeStruct((db, wb, kvw), F32), jax.ShapeDtypeStruct((db, wb, kvw), F32)),
        compiler_params=_params(("parallel",)), name="swa_sample")(sink, qz, kn, vn, buf_k, buf_v)


def _suffix_matrix():
    j = np.arange(LANES)[:, None]
    s = np.arange(LANES)[None, :]
    u = np.concatenate([(j > s).astype(np.float32), np.ones((LANES, LANES), np.float32)], axis=1)
    return jnp.asarray(np.concatenate([u, u], axis=0), dtype=BF16)


def _suffix_sums(lk, u):
    hi = lk.astype(BF16)
    lo = (lk - hi.astype(F32)).astype(BF16)
    return _dot(jnp.concatenate([hi, lo], axis=1), u)


def _attn_c_kernel(u_ref, q_ref, k_ref, v_ref, o_ref, carry_sc, acc_sc):
    qi = pl.program_id(2)
    ki = pl.program_id(3)
    scale = HD_C ** -0.5

    @pl.when(ki == 0)
    def _():
        carry_sc[...] = jnp.zeros_like(carry_sc)
        acc_sc[...] = jnp.zeros_like(acc_sc)

    def step(diag):
        k = k_ref[...]
        v = v_ref[...]
        u = u_ref[...]
        tq, tk = q_ref.shape[0], k.shape[0]
        if diag:
            keep = (lax.broadcasted_iota(jnp.int32, (tq, tk), 1)
                    < lax.broadcasted_iota(jnp.int32, (tq, tk), 0))
        for g in range(G_C):
            z = _dot_nt(q_ref[:, g * HD_C:(g + 1) * HD_C], k) * scale
            lb = _log_sigmoid(z)
            lk = lb - z
            if diag:
                lk = jnp.where(keep, lk, 0.0)
            carry = carry_sc[g]
            acc = acc_sc[g]
            for sb in reversed(range(tk // LANES)):
                cols = slice(sb * LANES, (sb + 1) * LANES)
                sums = _suffix_sums(lk[:, cols], u)
                a = jnp.exp(lb[:, cols] + carry + sums[:, :LANES])
                if diag:
                    a = jnp.where(keep[:, cols], a, 0.0)
                acc = acc + _dot(a.astype(BF16), v[cols, :])
                carry = carry + sums[:, LANES:]
            carry_sc[g] = carry
            acc_sc[g] = acc

    @pl.when(ki == 0)
    def _():
        step(True)

    @pl.when((ki > 0) & (ki <= qi))
    def _():
        step(False)

    @pl.when(ki == qi)
    def _():
        for g in range(G_C):
            o_ref[:, g * HD_C:(g + 1) * HD_C] = acc_sc[g].astype(o_ref.dtype)


def _attn_c_prompt(u, q, k, v, *, batch, seq, tq):
    nq = seq // tq
    gw = G_C * HD_C
    kmap = lambda b, h, i, j: (b * nq + jnp.maximum(i - j, 0), h)
    return pl.pallas_call(
        _attn_c_kernel, grid=(batch, KV_C, nq, nq),
        in_specs=[pl.BlockSpec((2 * LANES, 2 * LANES), lambda b, h, i, j: (0, 0)),
                  pl.BlockSpec((tq, gw), lambda b, h, i, j: (b * nq + i, h)),
                  pl.BlockSpec((tq, HD_C), kmap), pl.BlockSpec((tq, HD_C), kmap)],
        out_specs=pl.BlockSpec((tq, gw), lambda b, h, i, j: (b * nq + i, h)),
        out_shape=jax.ShapeDtypeStruct((batch * seq, H_C * HD_C), BF16),
        scratch_shapes=[pltpu.VMEM((G_C, tq, LANES), F32), pltpu.VMEM((G_C, tq, HD_C), F32)],
        compiler_params=_params(("parallel", "parallel", "parallel", "arbitrary")),
        name="sb_attn_prompt")(u, q, k, v)


def _attn_c_sample_kernel(pt_ref, u_ref, qz_ref, *rest, pps):
    k_refs, v_refs = rest[:pps], rest[pps:2 * pps]
    o_ref, carry_sc, acc_sc = rest[2 * pps:]
    st = pl.program_id(1)
    scale = HD_C ** -0.5

    @pl.when(st == 0)
    def _():
        carry_sc[...] = jnp.zeros_like(carry_sc)
        acc_sc[...] = jnp.zeros_like(acc_sc)

    u = u_ref[...]
    carry = carry_sc[...]
    acc = acc_sc[...]
    row_kv = lax.broadcasted_iota(jnp.int32, (H_C, LANES), 0) // G_C
    for i in reversed(range(pps)):
        z = jnp.zeros((H_C, k_refs[i].shape[0]), F32)
        for kv in range(KV_C):
            z = z + _dot_nt(qz_ref[0, kv], k_refs[i][:, kv, :].astype(BF16))
        z = z * scale
        lb = _log_sigmoid(z)
        sums = _suffix_sums(lb - z, u)
        a = jnp.exp(lb + carry + sums[:, :LANES])
        for kv in range(KV_C):
            a_kv = jnp.where(row_kv == kv, a, 0.0).astype(BF16)
            acc = acc + _dot(a_kv, v_refs[i][:, kv, :].astype(BF16))
        carry = carry + sums[:, LANES:]
    carry_sc[...] = carry
    acc_sc[...] = acc

    @pl.when(st == pl.num_programs(1) - 1)
    def _():
        o_ref[0] = acc


def _attn_c_sample(pt, u, qz, cache_k, cache_v, *, layer, pps):
    db = qz.shape[0]
    n_pages = pt.shape[0] // db
    n_steps = n_pages // pps
    page = cache_k.shape[2]

    def page_map(i):
        return lambda b, s, pt_ref: (layer, pt_ref[b * n_pages + (n_steps - 1 - s) * pps + i], 0, 0, 0)

    in_specs = [pl.BlockSpec((2 * LANES, 2 * LANES), lambda b, s, pt_ref: (0, 0)),
                pl.BlockSpec((1, KV_C, H_C, HD_C), lambda b, s, pt_ref: (b, 0, 0, 0))]
    in_specs += [pl.BlockSpec((None, None, page, KV_C, HD_C), page_map(i)) for i in range(pps)] * 2
    gs = pltpu.PrefetchScalarGridSpec(
        num_scalar_prefetch=1, grid=(db, n_steps), in_specs=in_specs,
        out_specs=pl.BlockSpec((1, H_C, HD_C), lambda b, s, pt_ref: (b, 0, 0)),
        scratch_shapes=[pltpu.VMEM((H_C, LANES), F32), pltpu.VMEM((H_C, HD_C), F32)])
    return pl.pallas_call(
        functools.partial(_attn_c_sample_kernel, pps=pps),
        grid_spec=gs, out_shape=jax.ShapeDtypeStruct((db, H_C, HD_C), F32),
        compiler_params=_params(("parallel", "arbitrary")), name="sb_attn_sample")(
            pt, u, qz, *([cache_k] * pps), *([cache_v] * pps))


def _pick(n, pref):
    return pref if n % pref == 0 else n


def kernel(x_prompt, x_sample, cache_a_k, cache_a_v, cache_c_k, cache_c_v, state_win_k, state_win_v, page_table, g_mix_pre, g_mix_post, g_ffn_pre, g_ffn_post, w_a_qkv, lam_q1, lam_k1, lam_q2, lam_k2, g_a_sub, w_a_out, w_b_qkv, b_sink, w_b_out, w_c_qkv, w_c_out, w_up, w_down):
    batch, seq, _ = x_prompt.shape
    db, dseq, _ = x_sample.shape
    assert dseq == 1
    depth = g_mix_pre.shape[0]
    n_pages = page_table.shape[1]
    page = cache_a_k.shape[2]
    past_len = n_pages * page
    m_p = batch * seq
    tm = _pick(m_p, 256)
    tq = _pick(seq, 512)
    tm_mlp = _pick(m_p, 512)
    pps = _pick(n_pages, 8)

    xp = x_prompt.reshape(m_p, D_MODEL)
    xs = x_sample.reshape(db, D_MODEL)
    pt = page_table.reshape(-1)
    pos_p = jnp.arange(seq, dtype=jnp.int32)
    pos_s = jnp.full((db,), past_len, dtype=jnp.int32)
    tabs = {hd: (_rope_tables(pos_p, hd), _rope_tables(pos_s, hd)) for hd in (HD_A, HD_B)}
    u = _suffix_matrix()

    def token_mixers(x, o, w_out, i, tm_o, tm_f):
        x = _outproj(o, w_out, x, g_mix_post[i], tm=tm_o)
        return _mlp(x, g_ffn_pre[i], w_up[i].astype(BF16), w_down[i].astype(BF16), g_ffn_post[i],
                    tm=tm_f, tf=512)

    outs = {name: [] for name in ("a_kp", "a_vp", "a_ks", "a_vs", "b_kp", "b_vp", "b_ks", "b_vs",
                                  "c_kp", "c_vp", "c_ks", "c_vs")}
    for i in range(depth):
        kind, j = i % N_MIXERS, i // N_MIXERS
        if kind == 0:
            lam_init = 0.8 - 0.6 * math.exp(-0.3 * i)
            w = w_a_qkv[j].astype(BF16)
            w_out = w_a_out[j].astype(BF16)
            lamvec = jnp.stack([lam_q1[j], lam_k1[j], lam_q2[j], lam_k2[j]])
            kw, vw = KV_A * 2 * HD_A, KV_A * VD_A
            q, k32, v32, k16, v16 = _qkv(xp, g_mix_pre[i], w, tabs[HD_A][0], rope_kind=HD_A, kw=kw, vw=vw, tm=tm)
            o = _attn_a_prompt(q, k16, v16, lamvec, g_a_sub[j], batch=batch, seq=seq, lam_init=lam_init, tq=tq)
            qs, ks32, vs32, _, _ = _qkv(xs, g_mix_pre[i], w, tabs[HD_A][1], rope_kind=HD_A, kw=kw, vw=vw, tm=db)
            q5 = qs.reshape(db, KV_A, G_A, 2, HD_A)
            zero = jnp.zeros((db, KV_A, G_A, HD_A), BF16)
            qz = jnp.concatenate([jnp.concatenate([q5[:, :, :, 0], zero], axis=-1),
                                  jnp.concatenate([zero, q5[:, :, :, 1]], axis=-1)], axis=2)
            os_ = _attn_a_sample(pt, lamvec, g_a_sub[j], qz, ks32.reshape(db, 1, kw), vs32.reshape(db, 1, vw),
                                 cache_a_k, cache_a_v, layer=j, lam_init=lam_init, pps=pps)
            os_ = os_.reshape(db, H_A * VD_A)
            outs["a_kp"].append(k32.reshape(batch, seq, KV_A, 2, HD_A))
            outs["a_vp"].append(v32.reshape(batch, seq, KV_A, VD_A))
            outs["a_ks"].append(ks32.reshape(db, 1, KV_A, 2, HD_A))
            outs["a_vs"].append(vs32.reshape(db, 1, KV_A, VD_A))
        elif kind == 1:
            w = w_b_qkv[j].astype(BF16)
            w_out = w_b_out[j].astype(BF16)
            kw = vw = KV_B * HD_B
            q, k32, v32, k16, v16 = _qkv(xp, g_mix_pre[i], w, tabs[HD_B][0], rope_kind=HD_B, kw=kw, vw=vw, tm=tm)
            o = _attn_b_prompt(b_sink[j], q, k16, v16, batch=batch, seq=seq)
            qs, ks32, vs32, _, _ = _qkv(xs, g_mix_pre[i], w, tabs[HD_B][1], rope_kind=HD_B, kw=kw, vw=vw, tm=db)
            q4 = qs.reshape(db, KV_B, G_B, HD_B)
            qz = jnp.stack([jnp.pad(q4[:, kv], ((0, 0), (0, 0), (kv * HD_B, (KV_B - 1 - kv) * HD_B)))
                            for kv in range(KV_B)], axis=1)
            wb = state_win_k.shape[2]
            os_, nk, nv = _attn_b_sample(b_sink[j].reshape(KV_B, G_B, 1), qz, ks32.reshape(db, 1, kw),
                                         vs32.reshape(db, 1, vw), state_win_k[j].reshape(db, wb, kw),
                                         state_win_v[j].reshape(db, wb, vw), past_len=past_len)
            os_ = os_.reshape(db, H_B * HD_B)
            outs["b_kp"].append(k32.reshape(batch, seq, KV_B, HD_B)[:, seq - WINDOW:])
            outs["b_vp"].append(v32.reshape(batch, seq, KV_B, HD_B)[:, seq - WINDOW:])
            outs["b_ks"].append(nk.reshape(db, wb, KV_B, HD_B))
            outs["b_vs"].append(nv.reshape(db, wb, KV_B, HD_B))
        else:
            w = w_c_qkv[j].astype(BF16)
            w_out = w_c_out[j].astype(BF16)
            kw = vw = KV_C * HD_C
            q, k32, v32, k16, v16 = _qkv(xp, g_mix_pre[i], w, None, rope_kind=None, kw=kw, vw=vw, tm=tm)
            o = _attn_c_prompt(u, q, k16, v16, batch=batch, seq=seq, tq=tq)
            qs, ks32, vs32, _, _ = _qkv(xs, g_mix_pre[i], w, None, rope_kind=None, kw=kw, vw=vw, tm=db)
            q4 = qs.reshape(db, KV_C, G_C, HD_C)
            qz = jnp.stack([jnp.pad(q4[:, kv], ((0, 0), (kv * G_C, (KV_C - 1 - kv) * G_C), (0, 0)))
                            for kv in range(KV_C)], axis=1)
            os_ = _attn_c_sample(pt, u, qz, cache_c_k, cache_c_v, layer=j, pps=pps).reshape(db, H_C * HD_C)
            outs["c_kp"].append(k32.reshape(batch, seq, KV_C, HD_C))
            outs["c_vp"].append(v32.reshape(batch, seq, KV_C, HD_C))
            outs["c_ks"].append(ks32.reshape(db, 1, KV_C, HD_C))
            outs["c_vs"].append(vs32.reshape(db, 1, KV_C, HD_C))
        xp = token_mixers(xp, o, w_out, i, tm, tm_mlp)
        xs = token_mixers(xs, os_, w_out, i, db, db)
    state = [jnp.stack(outs[name]) for name in ("a_kp", "a_vp", "a_ks", "a_vs", "b_kp", "b_vp", "b_ks", "b_vs",
                                               "c_kp", "c_vp", "c_ks", "c_vs")]
    return (xp.reshape(batch, seq, D_MODEL), xs.reshape(db, 1, D_MODEL), *state)
```

```python
import functools
import math

import numpy as np
import jax
import jax.numpy as jnp
from jax import lax
from jax.experimental import pallas as pl
from jax.experimental.pallas import tpu as pltpu

D_MODEL = 2048
N_MIXERS = 3
HD_A, KV_A = 128, 2
H_A = D_MODEL // (2 * HD_A)
G_A = H_A // KV_A
VD_A = 2 * HD_A
HD_B, KV_B = 64, 4
H_B = D_MODEL // HD_B
G_B = H_B // KV_B
WINDOW = 128
HD_C, KV_C = 128, 4
H_C = D_MODEL // HD_C
G_C = H_C // KV_C
ROPE_THETA = 10000.0
EPS = 1e-6

LANES = 128
VMEM_LIMIT = 56 * 1024 * 1024
F32 = jnp.float32
BF16 = jnp.bfloat16
NEG_INF = float("-inf")


def _params(sem):
    return pltpu.CompilerParams(dimension_semantics=sem, vmem_limit_bytes=VMEM_LIMIT)


def _rms(x, g):
    return x * lax.rsqrt(jnp.mean(x * x, axis=-1, keepdims=True) + EPS) * g


def _dot(a, b):
    return jnp.dot(a, b, preferred_element_type=F32)


def _dot_nt(a, b):
    return lax.dot_general(a, b, (((1,), (1,)), ((), ())), preferred_element_type=F32)


def _stick_logs(z2):
    w2 = jnp.maximum(z2, 0.0) + jnp.log2(1.0 + jnp.exp2(-jnp.abs(z2)))
    return z2 - w2, w2


def _qkv_kernel(*refs, rope_kind, kw, vw, q_scale, v_transposed):
    if rope_kind is None:
        x_ref, g_ref, w_ref, q_ref, k32_ref, v32_ref, k16_ref, v16_ref = refs
    else:
        x_ref, g_ref, w_ref, tab_ref, q_ref, k32_ref, v32_ref, k16_ref, v16_ref = refs
    xn = _rms(x_ref[...], g_ref[...]).astype(BF16)

    def rope(y):
        if rope_kind is None:
            return y
        if rope_kind == 128:
            return y * tab_ref[0] + pltpu.roll(y, 64, 1) * tab_ref[1]
        return (y * tab_ref[0] + pltpu.roll(y, 96, 1) * tab_ref[1]
                + pltpu.roll(y, 32, 1) * tab_ref[2])

    def project(c0, width, store):
        step = min(width, 4 * LANES)
        for s0 in range(0, width, step):
            y = _dot(xn, w_ref[:, c0 + s0:c0 + s0 + step])
            for t in range(step // LANES):
                store(s0 + t * LANES, y[:, t * LANES:(t + 1) * LANES])

    def store_q(c, y):
        y = rope(y)
        if q_scale != 1.0:
            y = y * q_scale
        q_ref[:, c:c + LANES] = y.astype(BF16)

    def store_k(c, y):
        y = rope(y)
        k32_ref[:, c:c + LANES] = y
        k16_ref[:, c:c + LANES] = y.astype(BF16)

    def store_v(c, y):
        v32_ref[:, c:c + LANES] = y
        if v_transposed:
            v16_ref[c:c + LANES, :] = y.T.astype(BF16)
        else:
            v16_ref[:, c:c + LANES] = y.astype(BF16)

    project(0, D_MODEL, store_q)
    project(D_MODEL, kw, store_k)
    project(D_MODEL + kw, vw, store_v)


def _qkv(x, g, w, tab, *, rope_kind, kw, vw, tm, q_scale, v_transposed=False):
    m = x.shape[0]
    n = w.shape[1]
    grid = (m // tm,)
    in_specs = [pl.BlockSpec((tm, D_MODEL), lambda i: (i, 0)),
                pl.BlockSpec((1, D_MODEL), lambda i: (0, 0)),
                pl.BlockSpec((D_MODEL, n), lambda i: (0, 0))]
    args = [x, g.reshape(1, D_MODEL), w]
    if rope_kind is not None:
        nt = tab.shape[1] // tm
        in_specs.append(pl.BlockSpec((3, tm, LANES), lambda i: (0, i % nt, 0)))
        args.append(tab)
    if v_transposed:
        v16_shape, v16_spec = (vw, m), pl.BlockSpec((vw, tm), lambda i: (0, i))
    else:
        v16_shape, v16_spec = (m, vw), pl.BlockSpec((tm, vw), lambda i: (i, 0))
    out_shape = (jax.ShapeDtypeStruct((m, D_MODEL), BF16),
                 jax.ShapeDtypeStruct((m, kw), F32), jax.ShapeDtypeStruct((m, vw), F32),
                 jax.ShapeDtypeStruct((m, kw), BF16), jax.ShapeDtypeStruct(v16_shape, BF16))
    out_specs = (pl.BlockSpec((tm, D_MODEL), lambda i: (i, 0)),
                 pl.BlockSpec((tm, kw), lambda i: (i, 0)), pl.BlockSpec((tm, vw), lambda i: (i, 0)),
                 pl.BlockSpec((tm, kw), lambda i: (i, 0)), v16_spec)
    return pl.pallas_call(
        functools.partial(_qkv_kernel, rope_kind=rope_kind, kw=kw, vw=vw, q_scale=q_scale,
                          v_transposed=v_transposed),
        grid=grid, in_specs=in_specs, out_specs=out_specs, out_shape=out_shape,
        compiler_params=_params(("parallel",)), name="qkv_proj")(*args)


def _rope_tables(pos, hd):
    half = hd // 2
    inv_freq = ROPE_THETA ** (-jnp.arange(half, dtype=F32) * 2.0 / hd)
    ang = pos.astype(F32)[:, None] * inv_freq[None, :]
    cos, sin = jnp.cos(ang), jnp.sin(ang)
    zero = jnp.zeros_like(sin)
    reps = LANES // hd
    cos_t = jnp.tile(jnp.concatenate([cos, cos], axis=-1), (1, reps))
    if hd == LANES:
        s1 = jnp.concatenate([-sin, sin], axis=-1)
        s2 = jnp.zeros_like(s1)
    else:
        s1 = jnp.tile(jnp.concatenate([-sin, zero], axis=-1), (1, reps))
        s2 = jnp.tile(jnp.concatenate([zero, sin], axis=-1), (1, reps))
    return jnp.stack([cos_t, s1, s2])


def _outproj_kernel(o_ref, w_ref, x_ref, g_ref, y_ref):
    y = _dot(o_ref[...].astype(BF16), w_ref[...])
    y_ref[...] = x_ref[...] + _rms(y, g_ref[...])


def _outproj(o, w, x, g, *, tm):
    m = x.shape[0]
    return pl.pallas_call(
        _outproj_kernel, grid=(m // tm,),
        in_specs=[pl.BlockSpec((tm, D_MODEL), lambda i: (i, 0)),
                  pl.BlockSpec((D_MODEL, D_MODEL), lambda i: (0, 0)),
                  pl.BlockSpec((tm, D_MODEL), lambda i: (i, 0)),
                  pl.BlockSpec((1, D_MODEL), lambda i: (0, 0))],
        out_specs=pl.BlockSpec((tm, D_MODEL), lambda i: (i, 0)),
        out_shape=jax.ShapeDtypeStruct((m, D_MODEL), F32),
        compiler_params=_params(("parallel",)), name="out_proj")(o, w, x, g.reshape(1, D_MODEL))


def _mlp_kernel(x_ref, gpre_ref, wup_ref, wdn_ref, gpost_ref, y_ref, xn_sc, acc_sc):
    f = pl.program_id(1)

    @pl.when(f == 0)
    def _():
        xn_sc[...] = _rms(x_ref[...], gpre_ref[...]).astype(BF16)
        acc_sc[...] = jnp.zeros_like(acc_sc)

    h = jnp.maximum(_dot(xn_sc[...], wup_ref[...]), 0.0)
    acc_sc[...] += _dot((h * h).astype(BF16), wdn_ref[...])

    @pl.when(f == pl.num_programs(1) - 1)
    def _():
        y_ref[...] = x_ref[...] + _rms(acc_sc[...], gpost_ref[...])


def _mlp(x, gpre, wup, wdn, gpost, *, tm, tf):
    m = x.shape[0]
    d_ff = wup.shape[1]
    return pl.pallas_call(
        _mlp_kernel, grid=(m // tm, d_ff // tf),
        in_specs=[pl.BlockSpec((tm, D_MODEL), lambda i, f: (i, 0)),
                  pl.BlockSpec((1, D_MODEL), lambda i, f: (0, 0)),
                  pl.BlockSpec((D_MODEL, tf), lambda i, f: (0, f)),
                  pl.BlockSpec((tf, D_MODEL), lambda i, f: (f, 0)),
                  pl.BlockSpec((1, D_MODEL), lambda i, f: (0, 0))],
        out_specs=pl.BlockSpec((tm, D_MODEL), lambda i, f: (i, 0)),
        out_shape=jax.ShapeDtypeStruct((m, D_MODEL), F32),
        scratch_shapes=[pltpu.VMEM((tm, D_MODEL), BF16), pltpu.VMEM((tm, D_MODEL), F32)],
        compiler_params=_params(("parallel", "arbitrary")), name="mlp")(
            x, gpre.reshape(1, D_MODEL), wup, wdn, gpost.reshape(1, D_MODEL))


def _diff_lambda(lam_ref, lam_init):
    d1 = jnp.sum(lam_ref[0:1, :] * lam_ref[1:2, :], axis=-1, keepdims=True)
    d2 = jnp.sum(lam_ref[2:3, :] * lam_ref[3:4, :], axis=-1, keepdims=True)
    return jnp.exp(d1) - jnp.exp(d2) + lam_init


def _attn_a_kernel(lam_ref, gsub_ref, q_ref, k_ref, vt_ref, o_ref, m_sc, l_sc, acc_sc, *, lam_init):
    qi = pl.program_id(2)
    ki = pl.program_id(3)

    @pl.when(ki == 0)
    def _():
        m_sc[...] = jnp.full_like(m_sc, NEG_INF)
        l_sc[...] = jnp.zeros_like(l_sc)
        acc_sc[...] = jnp.zeros_like(acc_sc)

    def step(diag):
        vt = vt_ref[...]
        tq, tk = q_ref.shape[0], k_ref.shape[0]
        if diag:
            keep = (lax.broadcasted_iota(jnp.int32, (tk, tq), 0)
                    <= lax.broadcasted_iota(jnp.int32, (tk, tq), 1))
        for g in range(G_A):
            for c in range(2):
                idx = g * 2 + c
                s = _dot_nt(k_ref[:, c * HD_A:(c + 1) * HD_A], q_ref[:, idx * HD_A:(idx + 1) * HD_A])
                if diag:
                    s = jnp.where(keep, s, NEG_INF)
                m_prev = m_sc[idx]
                m_new = jnp.maximum(m_prev, jnp.max(s, axis=0, keepdims=True))
                alpha = jnp.exp2(m_prev - m_new)
                p = jnp.exp2(s - m_new)
                l_sc[idx] = alpha * l_sc[idx] + jnp.sum(p, axis=0, keepdims=True)
                acc_sc[idx] = alpha * acc_sc[idx] + _dot(vt, p.astype(BF16))
                m_sc[idx] = m_new

    @pl.when(ki < qi)
    def _():
        step(False)

    @pl.when(ki == qi)
    def _():
        step(True)
        lam = _diff_lambda(lam_ref, lam_init)
        for g in range(G_A):
            ot = acc_sc[2 * g] / l_sc[2 * g] - lam * (acc_sc[2 * g + 1] / l_sc[2 * g + 1])
            o = _rms(ot.T, gsub_ref[...]) * (1.0 - lam_init)
            o_ref[:, g * VD_A:(g + 1) * VD_A] = o.astype(o_ref.dtype)


def _attn_a_prompt(q, k, vt, lamvec, gsub, *, batch, seq, lam_init, tq):
    nq = seq // tq
    gw = G_A * VD_A
    return pl.pallas_call(
        functools.partial(_attn_a_kernel, lam_init=lam_init),
        grid=(batch, KV_A, nq, nq),
        in_specs=[pl.BlockSpec((4, HD_A), lambda b, h, i, j: (0, 0)),
                  pl.BlockSpec((1, VD_A), lambda b, h, i, j: (0, 0)),
                  pl.BlockSpec((tq, gw), lambda b, h, i, j: (b * nq + i, h)),
                  pl.BlockSpec((tq, 2 * HD_A), lambda b, h, i, j: (b * nq + jnp.minimum(i, j), h)),
                  pl.BlockSpec((VD_A, tq), lambda b, h, i, j: (h, b * nq + jnp.minimum(i, j)))],
        out_specs=pl.BlockSpec((tq, gw), lambda b, h, i, j: (b * nq + i, h)),
        out_shape=jax.ShapeDtypeStruct((batch * seq, H_A * VD_A), BF16),
        scratch_shapes=[pltpu.VMEM((2 * G_A, 1, tq), F32), pltpu.VMEM((2 * G_A, 1, tq), F32),
                        pltpu.VMEM((2 * G_A, VD_A, tq), F32)],
        compiler_params=_params(("parallel", "parallel", "parallel", "arbitrary")),
        name="diff_attn_prompt")(lamvec, gsub.reshape(1, VD_A), q, k, vt)


def _page_rows(ref, offset):
    tokens = ref.shape[0]
    stride = math.prod(ref.shape[1:-1])
    flat = ref.reshape(tokens * stride, ref.shape[-1])
    return flat[pl.ds(offset, tokens, stride=stride), :]


def _attn_a_sample_kernel(pt_ref, lam_ref, gsub_ref, qz_ref, kn_ref, vn_ref, *rest, pps, lam_init):
    k_refs, v_refs = rest[:pps], rest[pps:3 * pps]
    o_ref, m_sc, l_sc, acc_sc = rest[3 * pps:]
    st = pl.program_id(1)

    @pl.when(st == 0)
    def _():
        for kv in range(KV_A):
            kn = kn_ref[0, :, kv * 2 * HD_A:(kv + 1) * 2 * HD_A].astype(BF16).astype(F32)
            vn = vn_ref[0, :, kv * VD_A:(kv + 1) * VD_A].astype(BF16).astype(F32)
            s = jnp.sum(qz_ref[0, kv].astype(F32) * kn, axis=-1, keepdims=True)
            m_sc[kv] = s
            l_sc[kv] = jnp.ones_like(s)
            acc_sc[kv] = jnp.broadcast_to(vn, acc_sc.shape[1:])

    for kv in range(KV_A):
        k_kv = jnp.concatenate(
            [jnp.concatenate([_page_rows(r, 2 * kv + c).astype(BF16) for c in range(2)], axis=1)
             for r in k_refs], axis=0)
        v_kv = jnp.concatenate(
            [jnp.concatenate([_page_rows(v_refs[2 * i + h], kv).astype(BF16) for h in range(2)], axis=1)
             for i in range(pps)], axis=0)
        s = _dot_nt(qz_ref[0, kv], k_kv)
        m_prev = m_sc[kv]
        m_new = jnp.maximum(m_prev, jnp.max(s, axis=-1, keepdims=True))
        alpha = jnp.exp2(m_prev - m_new)
        p = jnp.exp2(s - m_new)
        l_sc[kv] = alpha * l_sc[kv] + jnp.sum(p, axis=-1, keepdims=True)
        acc_sc[kv] = alpha * acc_sc[kv] + _dot(p.astype(BF16), v_kv)
        m_sc[kv] = m_new

    @pl.when(st == pl.num_programs(1) - 1)
    def _():
        lam = _diff_lambda(lam_ref, lam_init)
        for kv in range(KV_A):
            on = acc_sc[kv] / l_sc[kv]
            o = on[0:G_A] - lam * on[G_A:2 * G_A]
            o_ref[0, kv] = _rms(o, gsub_ref[...]) * (1.0 - lam_init)


def _attn_a_sample(pt, lamvec, gsub, qz, kn, vn, cache_k, cache_v, *, layer, lam_init, pps):
    db = qz.shape[0]
    n_pages = pt.shape[0] // db
    page = cache_k.shape[2]

    def page_map(i, tail):
        return lambda b, s, pt_ref: (layer, pt_ref[b * n_pages + s * pps + i]) + tail

    in_specs = [pl.BlockSpec((4, HD_A), lambda b, s, pt_ref: (0, 0)),
                pl.BlockSpec((1, VD_A), lambda b, s, pt_ref: (0, 0)),
                pl.BlockSpec((1, KV_A, 2 * G_A, 2 * HD_A), lambda b, s, pt_ref: (b, 0, 0, 0)),
                pl.BlockSpec((1, 1, KV_A * 2 * HD_A), lambda b, s, pt_ref: (b, 0, 0)),
                pl.BlockSpec((1, 1, KV_A * VD_A), lambda b, s, pt_ref: (b, 0, 0))]
    in_specs += [pl.BlockSpec((None, None, page, KV_A, 2, HD_A), page_map(i, (0, 0, 0, 0)))
                 for i in range(pps)]
    in_specs += [pl.BlockSpec((None, None, page, KV_A, LANES), page_map(i, (0, 0, h)))
                 for i in range(pps) for h in range(VD_A // LANES)]
    gs = pltpu.PrefetchScalarGridSpec(
        num_scalar_prefetch=1, grid=(db, n_pages // pps), in_specs=in_specs,
        out_specs=pl.BlockSpec((1, KV_A, G_A, VD_A), lambda b, s, pt_ref: (b, 0, 0, 0)),
        scratch_shapes=[pltpu.VMEM((KV_A, 2 * G_A, 1), F32), pltpu.VMEM((KV_A, 2 * G_A, 1), F32),
                        pltpu.VMEM((KV_A, 2 * G_A, VD_A), F32)])
    return pl.pallas_call(
        functools.partial(_attn_a_sample_kernel, pps=pps, lam_init=lam_init),
        grid_spec=gs, out_shape=jax.ShapeDtypeStruct((db, KV_A, G_A, VD_A), F32),
        compiler_params=_params(("parallel", "arbitrary")), name="diff_attn_sample")(
            pt, lamvec, gsub.reshape(1, VD_A), qz, kn, vn,
            *([cache_k] * pps), *([cache_v] * (2 * pps)))


def _attn_b_kernel(sink_ref, q_ref, kc_ref, kp_ref, vc_ref, vp_ref, o_ref):
    i = pl.program_id(1)
    w = q_ref.shape[0]
    scale = HD_B ** -0.5
    kk = jnp.concatenate([kp_ref[...], kc_ref[...]], axis=0)
    vv = jnp.concatenate([vp_ref[...], vc_ref[...]], axis=0)
    row = lax.broadcasted_iota(jnp.int32, (w, 2 * w), 0)
    col = lax.broadcasted_iota(jnp.int32, (w, 2 * w), 1)
    keep = (col >= row) & (col <= row + WINDOW) & ((i > 0) | (col >= w))
    for kv in range(KV_B):
        k = kk[:, kv * HD_B:(kv + 1) * HD_B]
        v = vv[:, kv * HD_B:(kv + 1) * HD_B]
        outs = []
        for g in range(G_B):
            h = kv * G_B + g
            q = q_ref[:, h * HD_B:(h + 1) * HD_B]
            s = jnp.where(keep, _dot_nt(q, k) * scale, NEG_INF)
            sink = sink_ref[h]
            mx = jnp.maximum(jnp.max(s, axis=-1, keepdims=True), sink)
            e = jnp.exp(s - mx)
            denom = jnp.sum(e, axis=-1, keepdims=True) + jnp.exp(sink - mx)
            outs.append(_dot(e.astype(BF16), v) / denom)
        for t in range(G_B // 2):
            h0 = kv * G_B + 2 * t
            o_ref[:, h0 * HD_B:(h0 + 2) * HD_B] = jnp.concatenate(
                [outs[2 * t], outs[2 * t + 1]], axis=1).astype(o_ref.dtype)


def _attn_b_prompt(sink, q, k, v, *, batch, seq):
    nb = seq // WINDOW
    kvw = KV_B * HD_B
    cur = lambda b, i: (b * nb + i, 0)
    prev = lambda b, i: (b * nb + jnp.maximum(i - 1, 0), 0)
    return pl.pallas_call(
        _attn_b_kernel, grid=(batch, nb),
        in_specs=[pl.BlockSpec(memory_space=pltpu.SMEM),
                  pl.BlockSpec((WINDOW, D_MODEL), cur),
                  pl.BlockSpec((WINDOW, kvw), cur), pl.BlockSpec((WINDOW, kvw), prev),
                  pl.BlockSpec((WINDOW, kvw), cur), pl.BlockSpec((WINDOW, kvw), prev)],
        out_specs=pl.BlockSpec((WINDOW, D_MODEL), cur),
        out_shape=jax.ShapeDtypeStruct((batch * seq, D_MODEL), BF16),
        compiler_params=_params(("parallel", "parallel")), name="swa_prompt")(sink, q, k, k, v, v)


def _attn_b_sample_kernel(sink_ref, qz_ref, kn_ref, vn_ref, bk_ref, bv_ref, o_ref, nk_ref, nv_ref,
                          *, past_len):
    wb = bk_ref.shape[1]
    scale = HD_B ** -0.5
    bk = bk_ref[0]
    bv = bv_ref[0]
    kn = kn_ref[0]
    vn = vn_ref[0]
    nk_ref[0, 0:wb - 1, :] = bk[1:wb, :]
    nk_ref[0, wb - 1:wb, :] = kn
    nv_ref[0, 0:wb - 1, :] = bv[1:wb, :]
    nv_ref[0, wb - 1:wb, :] = vn
    dist = wb - lax.broadcasted_iota(jnp.int32, (G_B, wb), 1)
    keep = (dist >= 0) & (dist <= WINDOW) & (past_len - dist >= 0)
    bk16 = bk.astype(BF16)
    bv16 = bv.astype(BF16)
    kn16 = kn.astype(BF16).astype(F32)
    vn16 = vn.astype(BF16).astype(F32)
    for kv in range(KV_B):
        qz = qz_ref[0, kv]
        s = jnp.where(keep, _dot_nt(qz, bk16) * scale, NEG_INF)
        s_new = jnp.sum(qz.astype(F32) * kn16, axis=-1, keepdims=True) * scale
        sink = sink_ref[kv]
        mx = jnp.maximum(jnp.maximum(jnp.max(s, axis=-1, keepdims=True), s_new), sink)
        e = jnp.exp(s - mx)
        e_new = jnp.exp(s_new - mx)
        denom = jnp.sum(e, axis=-1, keepdims=True) + e_new + jnp.exp(sink - mx)
        o = _dot(e.astype(BF16), bv16) + e_new.astype(BF16).astype(F32) * vn16
        o_ref[0, kv] = (o / denom)[:, kv * HD_B:(kv + 1) * HD_B]


def _attn_b_sample(sink, qz, kn, vn, buf_k, buf_v, *, past_len):
    db, wb, kvw = buf_k.shape
    row = lambda b: (b, 0, 0)
    return pl.pallas_call(
        functools.partial(_attn_b_sample_kernel, past_len=past_len), grid=(db,),
        in_specs=[pl.BlockSpec((KV_B, G_B, 1), lambda b: (0, 0, 0)),
                  pl.BlockSpec((1, KV_B, G_B, kvw), lambda b: (b, 0, 0, 0)),
                  pl.BlockSpec((1, 1, kvw), row), pl.BlockSpec((1, 1, kvw), row),
                  pl.BlockSpec((1, wb, kvw), row), pl.BlockSpec((1, wb, kvw), row)],
        out_specs=(pl.BlockSpec((1, KV_B, G_B, HD_B), lambda b: (b, 0, 0, 0)),
                   pl.BlockSpec((1, wb, kvw), row), pl.BlockSpec((1, wb, kvw), row)),
        out_shape=(jax.ShapeDtypeStruct((db, KV_B, G_B, HD_B), F32),
                   jax.ShapeDtypeStruct((db, wb, kvw), F32), jax.ShapeDtypeStruct((db, wb, kvw), F32)),
        compiler_params=_params(("parallel",)), name="swa_sample")(sink, qz, kn, vn, buf_k, buf_v)


def _suffix_matrix():
    j = np.arange(LANES)[:, None]
    s = np.arange(LANES)[None, :]
    u = np.concatenate([(j > s).astype(np.float32), np.ones((LANES, LANES), np.float32)], axis=1)
    return jnp.asarray(np.concatenate([u, u], axis=0), dtype=BF16)


def _suffix_sums(w, u):
    hi = w.astype(BF16)
    lo = (w - hi.astype(F32)).astype(BF16)
    return _dot(jnp.concatenate([hi, lo], axis=1), u)


def _attn_c_kernel(u_ref, q_ref, k_ref, v_ref, o_ref, carry_sc, acc_sc):
    qi = pl.program_id(2)
    ki = pl.program_id(3)

    @pl.when(ki == 0)
    def _():
        carry_sc[...] = jnp.zeros_like(carry_sc)
        acc_sc[...] = jnp.zeros_like(acc_sc)

    def step(diag):
        k = k_ref[...]
        v = v_ref[...]
        u = u_ref[...]
        tq, tk = q_ref.shape[0], k.shape[0]
        if diag:
            keep = (lax.broadcasted_iota(jnp.int32, (tq, tk), 1)
                    < lax.broadcasted_iota(jnp.int32, (tq, tk), 0))
        for g in range(G_C):
            z = _dot_nt(q_ref[:, g * HD_C:(g + 1) * HD_C], k)
            lb, w = _stick_logs(z)
            if diag:
                w = jnp.where(keep, w, 0.0)
            carry = carry_sc[g]
            acc = acc_sc[g]
            for sb in reversed(range(tk // LANES)):
                cols = slice(sb * LANES, (sb + 1) * LANES)
                sums = _suffix_sums(w[:, cols], u)
                a = jnp.exp2(lb[:, cols] - (carry + sums[:, :LANES]))
                if diag:
                    a = jnp.where(keep[:, cols], a, 0.0)
                acc = acc + _dot(a.astype(BF16), v[cols, :])
                carry = carry + sums[:, LANES:]
            carry_sc[g] = carry
            acc_sc[g] = acc

    @pl.when(ki == 0)
    def _():
        step(True)

    @pl.when((ki > 0) & (ki <= qi))
    def _():
        step(False)

    @pl.when(ki == qi)
    def _():
        for g in range(G_C):
            o_ref[:, g * HD_C:(g + 1) * HD_C] = acc_sc[g].astype(o_ref.dtype)


def _attn_c_prompt(u, q, k, v, *, batch, seq, tq):
    nq = seq // tq
    gw = G_C * HD_C
    kmap = lambda b, h, i, j: (b * nq + jnp.maximum(i - j, 0), h)
    return pl.pallas_call(
        _attn_c_kernel, grid=(batch, KV_C, nq, nq),
        in_specs=[pl.BlockSpec((2 * LANES, 2 * LANES), lambda b, h, i, j: (0, 0)),
                  pl.BlockSpec((tq, gw), lambda b, h, i, j: (b * nq + i, h)),
                  pl.BlockSpec((tq, HD_C), kmap), pl.BlockSpec((tq, HD_C), kmap)],
        out_specs=pl.BlockSpec((tq, gw), lambda b, h, i, j: (b * nq + i, h)),
        out_shape=jax.ShapeDtypeStruct((batch * seq, H_C * HD_C), BF16),
        scratch_shapes=[pltpu.VMEM((G_C, tq, LANES), F32), pltpu.VMEM((G_C, tq, HD_C), F32)],
        compiler_params=_params(("parallel", "parallel", "parallel", "arbitrary")),
        name="sb_attn_prompt")(u, q, k, v)


def _attn_c_sample_kernel(pt_ref, u_ref, qz_ref, *rest, pps):
    k_refs, v_refs = rest[:pps], rest[pps:2 * pps]
    o_ref, carry_sc, acc_sc = rest[2 * pps:]
    st = pl.program_id(1)

    @pl.when(st == 0)
    def _():
        carry_sc[...] = jnp.zeros_like(carry_sc)
        acc_sc[...] = jnp.zeros_like(acc_sc)

    page = k_refs[0].shape[0]
    z = None
    for kv in range(KV_C):
        k_kv = jnp.concatenate([_page_rows(r, kv).astype(BF16) for r in k_refs], axis=0)
        zk = _dot_nt(qz_ref[0, kv], k_kv)
        z = zk if z is None else z + zk
    lb, w = _stick_logs(z)
    w_rows = jnp.concatenate([w[:, i * page:(i + 1) * page] for i in range(pps)], axis=0)
    sums = _suffix_sums(w_rows, u_ref[...])
    carry = carry_sc[...]
    a_blocks = [None] * pps
    for i in reversed(range(pps)):
        rows = slice(i * H_C, (i + 1) * H_C)
        a_blocks[i] = jnp.exp2(lb[:, i * page:(i + 1) * page] - (carry + sums[rows, :LANES]))
        carry = carry + sums[rows, LANES:]
    carry_sc[...] = carry
    a = jnp.concatenate(a_blocks, axis=1)
    row_kv = lax.broadcasted_iota(jnp.int32, a.shape, 0) // G_C
    acc = acc_sc[...]
    for kv in range(KV_C):
        v_kv = jnp.concatenate([_page_rows(r, kv).astype(BF16) for r in v_refs], axis=0)
        acc = acc + _dot(jnp.where(row_kv == kv, a, 0.0).astype(BF16), v_kv)
    acc_sc[...] = acc

    @pl.when(st == pl.num_programs(1) - 1)
    def _():
        o_ref[0] = acc


def _attn_c_sample(pt, u, qz, cache_k, cache_v, *, layer, pps):
    db = qz.shape[0]
    n_pages = pt.shape[0] // db
    n_steps = n_pages // pps
    page = cache_k.shape[2]

    def page_map(i):
        return lambda b, s, pt_ref: (layer, pt_ref[b * n_pages + (n_steps - 1 - s) * pps + i], 0, 0, 0)

    in_specs = [pl.BlockSpec((2 * LANES, 2 * LANES), lambda b, s, pt_ref: (0, 0)),
                pl.BlockSpec((1, KV_C, H_C, HD_C), lambda b, s, pt_ref: (b, 0, 0, 0))]
    in_specs += [pl.BlockSpec((None, None, page, KV_C, HD_C), page_map(i)) for i in range(pps)] * 2
    gs = pltpu.PrefetchScalarGridSpec(
        num_scalar_prefetch=1, grid=(db, n_steps), in_specs=in_specs,
        out_specs=pl.BlockSpec((1, H_C, HD_C), lambda b, s, pt_ref: (b, 0, 0)),
        scratch_shapes=[pltpu.VMEM((H_C, LANES), F32), pltpu.VMEM((H_C, HD_C), F32)])
    return pl.pallas_call(
        functools.partial(_attn_c_sample_kernel, pps=pps),
        grid_spec=gs, out_shape=jax.ShapeDtypeStruct((db, H_C, HD_C), F32),
        compiler_params=_params(("parallel", "arbitrary")), name="sb_attn_sample")(
            pt, u, qz, *([cache_k] * pps), *([cache_v] * pps))


def _pick(n, pref):
    return pref if n % pref == 0 else n


def kernel(x_prompt, x_sample, cache_a_k, cache_a_v, cache_c_k, cache_c_v, state_win_k, state_win_v, page_table, g_mix_pre, g_mix_post, g_ffn_pre, g_ffn_post, w_a_qkv, lam_q1, lam_k1, lam_q2, lam_k2, g_a_sub, w_a_out, w_b_qkv, b_sink, w_b_out, w_c_qkv, w_c_out, w_up, w_down):
    batch, seq, _ = x_prompt.shape
    db, dseq, _ = x_sample.shape
    assert dseq == 1
    depth = g_mix_pre.shape[0]
    n_pages = page_table.shape[1]
    page = cache_a_k.shape[2]
    past_len = n_pages * page
    m_p = batch * seq
    tm = _pick(m_p, 256)
    tq = _pick(seq, 512)
    tm_mlp = _pick(m_p, 512)
    pps = _pick(n_pages, 16)
    scale_a = HD_A ** -0.5 * math.log2(math.e)
    scale_c = HD_C ** -0.5 * math.log2(math.e)

    xp = x_prompt.reshape(m_p, D_MODEL)
    xs = x_sample.reshape(db, D_MODEL)
    pt = page_table.reshape(-1)
    pos_p = jnp.arange(seq, dtype=jnp.int32)
    pos_s = jnp.full((db,), past_len, dtype=jnp.int32)
    tabs = {hd: (_rope_tables(pos_p, hd), _rope_tables(pos_s, hd)) for hd in (HD_A, HD_B)}
    u = _suffix_matrix()

    def token_mixers(x, o, w_out, i, tm_o, tm_f):
        x = _outproj(o, w_out, x, g_mix_post[i], tm=tm_o)
        return _mlp(x, g_ffn_pre[i], w_up[i].astype(BF16), w_down[i].astype(BF16), g_ffn_post[i],
                    tm=tm_f, tf=512)

    outs = {name: [] for name in ("a_kp", "a_vp", "a_ks", "a_vs", "b_kp", "b_vp", "b_ks", "b_vs",
                                  "c_kp", "c_vp", "c_ks", "c_vs")}
    for i in range(depth):
        kind, j = i % N_MIXERS, i // N_MIXERS
        if kind == 0:
            lam_init = 0.8 - 0.6 * math.exp(-0.3 * i)
            w = w_a_qkv[j].astype(BF16)
            w_out = w_a_out[j].astype(BF16)
            lamvec = jnp.stack([lam_q1[j], lam_k1[j], lam_q2[j], lam_k2[j]])
            kw, vw = KV_A * 2 * HD_A, KV_A * VD_A
            q, k32, v32, k16, v16t = _qkv(xp, g_mix_pre[i], w, tabs[HD_A][0], rope_kind=HD_A, kw=kw, vw=vw, tm=tm,
                                          q_scale=scale_a, v_transposed=True)
            o = _attn_a_prompt(q, k16, v16t, lamvec, g_a_sub[j], batch=batch, seq=seq, lam_init=lam_init, tq=tq)
            qs, ks32, vs32, _, _ = _qkv(xs, g_mix_pre[i], w, tabs[HD_A][1], rope_kind=HD_A, kw=kw, vw=vw, tm=db,
                                        q_scale=scale_a)
            q5 = qs.reshape(db, KV_A, G_A, 2, HD_A)
            zero = jnp.zeros((db, KV_A, G_A, HD_A), BF16)
            qz = jnp.concatenate([jnp.concatenate([q5[:, :, :, 0], zero], axis=-1),
                                  jnp.concatenate([zero, q5[:, :, :, 1]], axis=-1)], axis=2)
            os_ = _attn_a_sample(pt, lamvec, g_a_sub[j], qz, ks32.reshape(db, 1, kw), vs32.reshape(db, 1, vw),
                                 cache_a_k, cache_a_v, layer=j, lam_init=lam_init, pps=pps)
            os_ = os_.reshape(db, H_A * VD_A)
            outs["a_kp"].append(k32.reshape(batch, seq, KV_A, 2, HD_A))
            outs["a_vp"].append(v32.reshape(batch, seq, KV_A, VD_A))
            outs["a_ks"].append(ks32.reshape(db, 1, KV_A, 2, HD_A))
            outs["a_vs"].append(vs32.reshape(db, 1, KV_A, VD_A))
        elif kind == 1:
            w = w_b_qkv[j].astype(BF16)
            w_out = w_b_out[j].astype(BF16)
            kw = vw = KV_B * HD_B
            q, k32, v32, k16, v16 = _qkv(xp, g_mix_pre[i], w, tabs[HD_B][0], rope_kind=HD_B, kw=kw, vw=vw, tm=tm,
                                         q_scale=1.0)
            o = _attn_b_prompt(b_sink[j], q, k16, v16, batch=batch, seq=seq)
            qs, ks32, vs32, _, _ = _qkv(xs, g_mix_pre[i], w, tabs[HD_B][1], rope_kind=HD_B, kw=kw, vw=vw, tm=db,
                                        q_scale=1.0)
            q4 = qs.reshape(db, KV_B, G_B, HD_B)
            qz = jnp.stack([jnp.pad(q4[:, kv], ((0, 0), (0, 0), (kv * HD_B, (KV_B - 1 - kv) * HD_B)))
                            for kv in range(KV_B)], axis=1)
            wb = state_win_k.shape[2]
            os_, nk, nv = _attn_b_sample(b_sink[j].reshape(KV_B, G_B, 1), qz, ks32.reshape(db, 1, kw),
                                         vs32.reshape(db, 1, vw), state_win_k[j].reshape(db, wb, kw),
                                         state_win_v[j].reshape(db, wb, vw), past_len=past_len)
            os_ = os_.reshape(db, H_B * HD_B)
            outs["b_kp"].append(k32.reshape(batch, seq, KV_B, HD_B)[:, seq - WINDOW:])
            outs["b_vp"].append(v32.reshape(batch, seq, KV_B, HD_B)[:, seq - WINDOW:])
            outs["b_ks"].append(nk.reshape(db, wb, KV_B, HD_B))
            outs["b_vs"].append(nv.reshape(db, wb, KV_B, HD_B))
        else:
            w = w_c_qkv[j].astype(BF16)
            w_out = w_c_out[j].astype(BF16)
            kw = vw = KV_C * HD_C
            q, k32, v32, k16, v16 = _qkv(xp, g_mix_pre[i], w, None, rope_kind=None, kw=kw, vw=vw, tm=tm,
                                         q_scale=scale_c)
            o = _attn_c_prompt(u, q, k16, v16, batch=batch, seq=seq, tq=tq)
            qs, ks32, vs32, _, _ = _qkv(xs, g_mix_pre[i], w, None, rope_kind=None, kw=kw, vw=vw, tm=db,
                                        q_scale=scale_c)
            q4 = qs.reshape(db, KV_C, G_C, HD_C)
            qz = jnp.stack([jnp.pad(q4[:, kv], ((0, 0), (kv * G_C, (KV_C - 1 - kv) * G_C), (0, 0)))
                            for kv in range(KV_C)], axis=1)
            os_ = _attn_c_sample(pt, u, qz, cache_c_k, cache_c_v, layer=j, pps=pps).reshape(db, H_C * HD_C)
            outs["c_kp"].append(k32.reshape(batch, seq, KV_C, HD_C))
            outs["c_vp"].append(v32.reshape(batch, seq, KV_C, HD_C))
            outs["c_ks"].append(ks32.reshape(db, 1, KV_C, HD_C))
            outs["c_vs"].append(vs32.reshape(db, 1, KV_C, HD_C))
        xp = token_mixers(xp, o, w_out, i, tm, tm_mlp)
        xs = token_mixers(xs, os_, w_out, i, db, db)
    state = [jnp.stack(outs[name]) for name in ("a_kp", "a_vp", "a_ks", "a_vs", "b_kp", "b_vp", "b_ks", "b_vs",
                                               "c_kp", "c_vp", "c_ks", "c_vs")]
    return (xp.reshape(batch, seq, D_MODEL), xs.reshape(db, 1, D_MODEL), *state)
```

```python
import functools
import math

import numpy as np
import jax
import jax.numpy as jnp
from jax import lax
from jax.experimental import pallas as pl
from jax.experimental.pallas import tpu as pltpu

D_MODEL = 2048
N_MIXERS = 3
HD_A, KV_A = 128, 2
H_A = D_MODEL // (2 * HD_A)
G_A = H_A // KV_A
VD_A = 2 * HD_A
HD_B, KV_B = 64, 4
H_B = D_MODEL // HD_B
G_B = H_B // KV_B
WINDOW = 128
HD_C, KV_C = 128, 4
H_C = D_MODEL // HD_C
G_C = H_C // KV_C
ROPE_THETA = 10000.0
EPS = 1e-6

LANES = 128
VMEM_LIMIT = 56 * 1024 * 1024
F32 = jnp.float32
BF16 = jnp.bfloat16
NEG_INF = float("-inf")
LOGITS_AHEAD = 2


def _params(sem):
    return pltpu.CompilerParams(dimension_semantics=sem, vmem_limit_bytes=VMEM_LIMIT)


def _rms(x, g):
    return x * lax.rsqrt(jnp.mean(x * x, axis=-1, keepdims=True) + EPS) * g


def _dot(a, b):
    return jnp.dot(a, b, preferred_element_type=F32)


def _dot_nt(a, b):
    return lax.dot_general(a, b, (((1,), (1,)), ((), ())), preferred_element_type=F32)


def _stick_logs(z2):
    w2 = jnp.maximum(z2, 0.0) + jnp.log2(1.0 + jnp.exp2(-jnp.abs(z2)))
    return z2 - w2, w2


def _qkv_kernel(*refs, rope_kind, kw, vw, q_scale, v_transposed):
    if rope_kind is None:
        x_ref, g_ref, w_ref, q_ref, k32_ref, v32_ref, k16_ref, v16_ref = refs
    else:
        x_ref, g_ref, w_ref, tab_ref, q_ref, k32_ref, v32_ref, k16_ref, v16_ref = refs
    xn = _rms(x_ref[...], g_ref[...]).astype(BF16)

    def rope(y):
        if rope_kind is None:
            return y
        if rope_kind == 128:
            return y * tab_ref[0] + pltpu.roll(y, 64, 1) * tab_ref[1]
        return (y * tab_ref[0] + pltpu.roll(y, 96, 1) * tab_ref[1]
                + pltpu.roll(y, 32, 1) * tab_ref[2])

    def project(c0, width, store):
        step = min(width, 4 * LANES)
        for s0 in range(0, width, step):
            y = _dot(xn, w_ref[:, c0 + s0:c0 + s0 + step])
            for t in range(step // LANES):
                store(s0 + t * LANES, y[:, t * LANES:(t + 1) * LANES])

    def store_q(c, y):
        y = rope(y)
        if q_scale != 1.0:
            y = y * q_scale
        q_ref[:, c:c + LANES] = y.astype(BF16)

    def store_k(c, y):
        y = rope(y)
        k32_ref[:, c:c + LANES] = y
        k16_ref[:, c:c + LANES] = y.astype(BF16)

    def store_v(c, y):
        v32_ref[:, c:c + LANES] = y
        if v_transposed:
            v16_ref[c:c + LANES, :] = y.T.astype(BF16)
        else:
            v16_ref[:, c:c + LANES] = y.astype(BF16)

    project(0, D_MODEL, store_q)
    project(D_MODEL, kw, store_k)
    project(D_MODEL + kw, vw, store_v)


def _qkv(x, g, w, layer, tab, *, rope_kind, kw, vw, tm, q_scale, v_transposed=False):
    m = x.shape[0]
    n = w.shape[2]
    grid = (m // tm,)
    in_specs = [pl.BlockSpec((tm, D_MODEL), lambda i: (i, 0)),
                pl.BlockSpec((1, D_MODEL), lambda i: (0, 0)),
                pl.BlockSpec((None, D_MODEL, n), lambda i: (layer, 0, 0))]
    args = [x, g.reshape(1, D_MODEL), w]
    if rope_kind is not None:
        nt = tab.shape[1] // tm
        in_specs.append(pl.BlockSpec((3, tm, LANES), lambda i: (0, i % nt, 0)))
        args.append(tab)
    if v_transposed:
        v16_shape, v16_spec = (vw, m), pl.BlockSpec((vw, tm), lambda i: (0, i))
    else:
        v16_shape, v16_spec = (m, vw), pl.BlockSpec((tm, vw), lambda i: (i, 0))
    out_shape = (jax.ShapeDtypeStruct((m, D_MODEL), BF16),
                 jax.ShapeDtypeStruct((m, kw), F32), jax.ShapeDtypeStruct((m, vw), F32),
                 jax.ShapeDtypeStruct((m, kw), BF16), jax.ShapeDtypeStruct(v16_shape, BF16))
    out_specs = (pl.BlockSpec((tm, D_MODEL), lambda i: (i, 0)),
                 pl.BlockSpec((tm, kw), lambda i: (i, 0)), pl.BlockSpec((tm, vw), lambda i: (i, 0)),
                 pl.BlockSpec((tm, kw), lambda i: (i, 0)), v16_spec)
    return pl.pallas_call(
        functools.partial(_qkv_kernel, rope_kind=rope_kind, kw=kw, vw=vw, q_scale=q_scale,
                          v_transposed=v_transposed),
        grid=grid, in_specs=in_specs, out_specs=out_specs, out_shape=out_shape,
        compiler_params=_params(("parallel",)), name="qkv_proj")(*args)


def _rope_tables(pos, hd):
    half = hd // 2
    inv_freq = ROPE_THETA ** (-jnp.arange(half, dtype=F32) * 2.0 / hd)
    ang = pos.astype(F32)[:, None] * inv_freq[None, :]
    cos, sin = jnp.cos(ang), jnp.sin(ang)
    zero = jnp.zeros_like(sin)
    reps = LANES // hd
    cos_t = jnp.tile(jnp.concatenate([cos, cos], axis=-1), (1, reps))
    if hd == LANES:
        s1 = jnp.concatenate([-sin, sin], axis=-1)
        s2 = jnp.zeros_like(s1)
    else:
        s1 = jnp.tile(jnp.concatenate([-sin, zero], axis=-1), (1, reps))
        s2 = jnp.tile(jnp.concatenate([zero, sin], axis=-1), (1, reps))
    return jnp.stack([cos_t, s1, s2])


def _outproj_kernel(o_ref, w_ref, x_ref, g_ref, y_ref):
    y = _dot(o_ref[...].astype(BF16), w_ref[...])
    y_ref[...] = x_ref[...] + _rms(y, g_ref[...])


def _outproj(o, w, layer, x, g, *, tm):
    m = x.shape[0]
    return pl.pallas_call(
        _outproj_kernel, grid=(m // tm,),
        in_specs=[pl.BlockSpec((tm, D_MODEL), lambda i: (i, 0)),
                  pl.BlockSpec((None, D_MODEL, D_MODEL), lambda i: (layer, 0, 0)),
                  pl.BlockSpec((tm, D_MODEL), lambda i: (i, 0)),
                  pl.BlockSpec((1, D_MODEL), lambda i: (0, 0))],
        out_specs=pl.BlockSpec((tm, D_MODEL), lambda i: (i, 0)),
        out_shape=jax.ShapeDtypeStruct((m, D_MODEL), F32),
        compiler_params=_params(("parallel",)), name="out_proj")(o, w, x, g.reshape(1, D_MODEL))


def _mlp_kernel(x_ref, gpre_ref, wup_ref, wdn_ref, gpost_ref, y_ref, xn_sc, acc_sc):
    f = pl.program_id(1)

    @pl.when(f == 0)
    def _():
        xn_sc[...] = _rms(x_ref[...], gpre_ref[...]).astype(BF16)
        acc_sc[...] = jnp.zeros_like(acc_sc)

    h = jnp.maximum(_dot(xn_sc[...], wup_ref[...]), 0.0)
    acc_sc[...] += _dot((h * h).astype(BF16), wdn_ref[...])

    @pl.when(f == pl.num_programs(1) - 1)
    def _():
        y_ref[...] = x_ref[...] + _rms(acc_sc[...], gpost_ref[...])


def _mlp(x, gpre, wup, wdn, layer, gpost, *, tm, tf):
    m = x.shape[0]
    d_ff = wup.shape[2]
    return pl.pallas_call(
        _mlp_kernel, grid=(m // tm, d_ff // tf),
        in_specs=[pl.BlockSpec((tm, D_MODEL), lambda i, f: (i, 0)),
                  pl.BlockSpec((1, D_MODEL), lambda i, f: (0, 0)),
                  pl.BlockSpec((None, D_MODEL, tf), lambda i, f: (layer, 0, f)),
                  pl.BlockSpec((None, tf, D_MODEL), lambda i, f: (layer, f, 0)),
                  pl.BlockSpec((1, D_MODEL), lambda i, f: (0, 0))],
        out_specs=pl.BlockSpec((tm, D_MODEL), lambda i, f: (i, 0)),
        out_shape=jax.ShapeDtypeStruct((m, D_MODEL), F32),
        scratch_shapes=[pltpu.VMEM((tm, D_MODEL), BF16), pltpu.VMEM((tm, D_MODEL), F32)],
        compiler_params=_params(("parallel", "arbitrary")), name="mlp")(
            x, gpre.reshape(1, D_MODEL), wup, wdn, gpost.reshape(1, D_MODEL))


def _diff_lambda(lam_ref, lam_init):
    d1 = jnp.sum(lam_ref[0:1, :] * lam_ref[1:2, :], axis=-1, keepdims=True)
    d2 = jnp.sum(lam_ref[2:3, :] * lam_ref[3:4, :], axis=-1, keepdims=True)
    return jnp.exp(d1) - jnp.exp(d2) + lam_init


def _attn_a_kernel(lam_ref, gsub_ref, q_ref, k_ref, vt_ref, o_ref, m_sc, l_sc, acc_sc, *, lam_init):
    qi = pl.program_id(2)
    ki = pl.program_id(3)

    @pl.when(ki == 0)
    def _():
        m_sc[...] = jnp.full_like(m_sc, NEG_INF)
        l_sc[...] = jnp.zeros_like(l_sc)
        acc_sc[...] = jnp.zeros_like(acc_sc)

    def step(diag):
        vt = vt_ref[...]
        tq, tk = q_ref.shape[0], k_ref.shape[0]
        if diag:
            keep = (lax.broadcasted_iota(jnp.int32, (tk, tq), 0)
                    <= lax.broadcasted_iota(jnp.int32, (tk, tq), 1))
        def logits(idx):
            c = idx % 2
            return _dot_nt(k_ref[:, c * HD_A:(c + 1) * HD_A], q_ref[:, idx * HD_A:(idx + 1) * HD_A])

        queue = [logits(idx) for idx in range(LOGITS_AHEAD)]
        for idx in range(2 * G_A):
            s = queue.pop(0)
            if idx + LOGITS_AHEAD < 2 * G_A:
                queue.append(logits(idx + LOGITS_AHEAD))
            if diag:
                s = jnp.where(keep, s, NEG_INF)
            m_prev = m_sc[idx]
            m_new = jnp.maximum(m_prev, jnp.max(s, axis=0, keepdims=True))
            alpha = jnp.exp2(m_prev - m_new)
            p = jnp.exp2(s - m_new)
            l_sc[idx] = alpha * l_sc[idx] + jnp.sum(p, axis=0, keepdims=True)
            acc_sc[idx] = alpha * acc_sc[idx] + _dot(vt, p.astype(BF16))
            m_sc[idx] = m_new

    @pl.when(ki < qi)
    def _():
        step(False)

    @pl.when(ki == qi)
    def _():
        step(True)
        lam = _diff_lambda(lam_ref, lam_init)
        for g in range(G_A):
            ot = acc_sc[2 * g] / l_sc[2 * g] - lam * (acc_sc[2 * g + 1] / l_sc[2 * g + 1])
            o = _rms(ot.T, gsub_ref[...]) * (1.0 - lam_init)
            o_ref[:, g * VD_A:(g + 1) * VD_A] = o.astype(o_ref.dtype)


def _attn_a_prompt(q, k, vt, lamvec, gsub, *, batch, seq, lam_init, tq):
    nq = seq // tq
    gw = G_A * VD_A
    return pl.pallas_call(
        functools.partial(_attn_a_kernel, lam_init=lam_init),
        grid=(batch, KV_A, nq, nq),
        in_specs=[pl.BlockSpec((4, HD_A), lambda b, h, i, j: (0, 0)),
                  pl.BlockSpec((1, VD_A), lambda b, h, i, j: (0, 0)),
                  pl.BlockSpec((tq, gw), lambda b, h, i, j: (b * nq + i, h)),
                  pl.BlockSpec((tq, 2 * HD_A), lambda b, h, i, j: (b * nq + jnp.minimum(i, j), h)),
                  pl.BlockSpec((VD_A, tq), lambda b, h, i, j: (h, b * nq + jnp.minimum(i, j)))],
        out_specs=pl.BlockSpec((tq, gw), lambda b, h, i, j: (b * nq + i, h)),
        out_shape=jax.ShapeDtypeStruct((batch * seq, H_A * VD_A), BF16),
        scratch_shapes=[pltpu.VMEM((2 * G_A, 1, tq), F32), pltpu.VMEM((2 * G_A, 1, tq), F32),
                        pltpu.VMEM((2 * G_A, VD_A, tq), F32)],
        compiler_params=_params(("parallel", "parallel", "parallel", "arbitrary")),
        name="diff_attn_prompt")(lamvec, gsub.reshape(1, VD_A), q, k, vt)


def _page_rows(ref, offset):
    tokens = ref.shape[0]
    stride = math.prod(ref.shape[1:-1])
    flat = ref.reshape(tokens * stride, ref.shape[-1])
    return flat[pl.ds(offset, tokens, stride=stride), :]


def _attn_a_sample_kernel(pt_ref, lam_ref, gsub_ref, qz_ref, kn_ref, vn_ref, *rest, pps, lam_init):
    k_refs, v_refs = rest[:pps], rest[pps:3 * pps]
    o_ref, m_sc, l_sc, acc_sc = rest[3 * pps:]
    st = pl.program_id(1)

    @pl.when(st == 0)
    def _():
        for kv in range(KV_A):
            kn = kn_ref[0, :, kv * 2 * HD_A:(kv + 1) * 2 * HD_A].astype(BF16).astype(F32)
            vn = vn_ref[0, :, kv * VD_A:(kv + 1) * VD_A].astype(BF16).astype(F32)
            s = jnp.sum(qz_ref[0, kv].astype(F32) * kn, axis=-1, keepdims=True)
            m_sc[kv] = s
            l_sc[kv] = jnp.ones_like(s)
            acc_sc[kv] = jnp.broadcast_to(vn, acc_sc.shape[1:])

    for kv in range(KV_A):
        k_kv = jnp.concatenate(
            [jnp.concatenate([_page_rows(r, 2 * kv + c).astype(BF16) for c in range(2)], axis=1)
             for r in k_refs], axis=0)
        v_kv = jnp.concatenate(
            [jnp.concatenate([_page_rows(v_refs[2 * i + h], kv).astype(BF16) for h in range(2)], axis=1)
             for i in range(pps)], axis=0)
        s = _dot_nt(qz_ref[0, kv], k_kv)
        m_prev = m_sc[kv]
        m_new = jnp.maximum(m_prev, jnp.max(s, axis=-1, keepdims=True))
        alpha = jnp.exp2(m_prev - m_new)
        p = jnp.exp2(s - m_new)
        l_sc[kv] = alpha * l_sc[kv] + jnp.sum(p, axis=-1, keepdims=True)
        acc_sc[kv] = alpha * acc_sc[kv] + _dot(p.astype(BF16), v_kv)
        m_sc[kv] = m_new

    @pl.when(st == pl.num_programs(1) - 1)
    def _():
        lam = _diff_lambda(lam_ref, lam_init)
        for kv in range(KV_A):
            on = acc_sc[kv] / l_sc[kv]
            o = on[0:G_A] - lam * on[G_A:2 * G_A]
            o_ref[0, kv] = _rms(o, gsub_ref[...]) * (1.0 - lam_init)


def _attn_a_sample(pt, lamvec, gsub, qz, kn, vn, cache_k, cache_v, *, layer, lam_init, pps):
    db = qz.shape[0]
    n_pages = pt.shape[0] // db
    page = cache_k.shape[2]

    def page_map(i, tail):
        return lambda b, s, pt_ref: (layer, pt_ref[b * n_pages + s * pps + i]) + tail

    in_specs = [pl.BlockSpec((4, HD_A), lambda b, s, pt_ref: (0, 0)),
                pl.BlockSpec((1, VD_A), lambda b, s, pt_ref: (0, 0)),
                pl.BlockSpec((1, KV_A, 2 * G_A, 2 * HD_A), lambda b, s, pt_ref: (b, 0, 0, 0)),
                pl.BlockSpec((1, 1, KV_A * 2 * HD_A), lambda b, s, pt_ref: (b, 0, 0)),
                pl.BlockSpec((1, 1, KV_A * VD_A), lambda b, s, pt_ref: (b, 0, 0))]
    in_specs += [pl.BlockSpec((None, None, page, KV_A, 2, HD_A), page_map(i, (0, 0, 0, 0)))
                 for i in range(pps)]
    in_specs += [pl.BlockSpec((None, None, page, KV_A, LANES), page_map(i, (0, 0, h)))
                 for i in range(pps) for h in range(VD_A // LANES)]
    gs = pltpu.PrefetchScalarGridSpec(
        num_scalar_prefetch=1, grid=(db, n_pages // pps), in_specs=in_specs,
        out_specs=pl.BlockSpec((1, KV_A, G_A, VD_A), lambda b, s, pt_ref: (b, 0, 0, 0)),
        scratch_shapes=[pltpu.VMEM((KV_A, 2 * G_A, 1), F32), pltpu.VMEM((KV_A, 2 * G_A, 1), F32),
                        pltpu.VMEM((KV_A, 2 * G_A, VD_A), F32)])
    return pl.pallas_call(
        functools.partial(_attn_a_sample_kernel, pps=pps, lam_init=lam_init),
        grid_spec=gs, out_shape=jax.ShapeDtypeStruct((db, KV_A, G_A, VD_A), F32),
        compiler_params=_params(("parallel", "arbitrary")), name="diff_attn_sample")(
            pt, lamvec, gsub.reshape(1, VD_A), qz, kn, vn,
            *([cache_k] * pps), *([cache_v] * (2 * pps)))


def _attn_b_kernel(sink_ref, q_ref, kc_ref, kp_ref, vc_ref, vp_ref, o_ref):
    i = pl.program_id(1)
    w = q_ref.shape[0]
    scale = HD_B ** -0.5
    kk = jnp.concatenate([kp_ref[...], kc_ref[...]], axis=0)
    vv = jnp.concatenate([vp_ref[...], vc_ref[...]], axis=0)
    row = lax.broadcasted_iota(jnp.int32, (w, 2 * w), 0)
    col = lax.broadcasted_iota(jnp.int32, (w, 2 * w), 1)
    keep = (col >= row) & (col <= row + WINDOW) & ((i > 0) | (col >= w))
    for kv in range(KV_B):
        k = kk[:, kv * HD_B:(kv + 1) * HD_B]
        v = vv[:, kv * HD_B:(kv + 1) * HD_B]
        outs = []
        for g in range(G_B):
            h = kv * G_B + g
            q = q_ref[:, h * HD_B:(h + 1) * HD_B]
            s = jnp.where(keep, _dot_nt(q, k) * scale, NEG_INF)
            sink = sink_ref[h]
            mx = jnp.maximum(jnp.max(s, axis=-1, keepdims=True), sink)
            e = jnp.exp(s - mx)
            denom = jnp.sum(e, axis=-1, keepdims=True) + jnp.exp(sink - mx)
            outs.append(_dot(e.astype(BF16), v) / denom)
        for t in range(G_B // 2):
            h0 = kv * G_B + 2 * t
            o_ref[:, h0 * HD_B:(h0 + 2) * HD_B] = jnp.concatenate(
                [outs[2 * t], outs[2 * t + 1]], axis=1).astype(o_ref.dtype)


def _attn_b_prompt(sink, q, k, v, *, batch, seq):
    nb = seq // WINDOW
    kvw = KV_B * HD_B
    cur = lambda b, i: (b * nb + i, 0)
    prev = lambda b, i: (b * nb + jnp.maximum(i - 1, 0), 0)
    return pl.pallas_call(
        _attn_b_kernel, grid=(batch, nb),
        in_specs=[pl.BlockSpec(memory_space=pltpu.SMEM),
                  pl.BlockSpec((WINDOW, D_MODEL), cur),
                  pl.BlockSpec((WINDOW, kvw), cur), pl.BlockSpec((WINDOW, kvw), prev),
                  pl.BlockSpec((WINDOW, kvw), cur), pl.BlockSpec((WINDOW, kvw), prev)],
        out_specs=pl.BlockSpec((WINDOW, D_MODEL), cur),
        out_shape=jax.ShapeDtypeStruct((batch * seq, D_MODEL), BF16),
        compiler_params=_params(("parallel", "parallel")), name="swa_prompt")(sink, q, k, k, v, v)


def _attn_b_sample_kernel(sink_ref, qz_ref, kn_ref, vn_ref, bk_ref, bv_ref, o_ref, nk_ref, nv_ref,
                          *, past_len):
    wb = bk_ref.shape[1]
    scale = HD_B ** -0.5
    bk = bk_ref[0]
    bv = bv_ref[0]
    kn = kn_ref[0]
    vn = vn_ref[0]
    nk_ref[0, 0:wb - 1, :] = bk[1:wb, :]
    nk_ref[0, wb - 1:wb, :] = kn
    nv_ref[0, 0:wb - 1, :] = bv[1:wb, :]
    nv_ref[0, wb - 1:wb, :] = vn
    dist = wb - lax.broadcasted_iota(jnp.int32, (G_B, wb), 1)
    keep = (dist >= 0) & (dist <= WINDOW) & (past_len - dist >= 0)
    bk16 = bk.astype(BF16)
    bv16 = bv.astype(BF16)
    kn16 = kn.astype(BF16).astype(F32)
    vn16 = vn.astype(BF16).astype(F32)
    for kv in range(KV_B):
        qz = qz_ref[0, kv]
        s = jnp.where(keep, _dot_nt(qz, bk16) * scale, NEG_INF)
        s_new = jnp.sum(qz.astype(F32) * kn16, axis=-1, keepdims=True) * scale
        sink = sink_ref[kv]
        mx = jnp.maximum(jnp.maximum(jnp.max(s, axis=-1, keepdims=True), s_new), sink)
        e = jnp.exp(s - mx)
        e_new = jnp.exp(s_new - mx)
        denom = jnp.sum(e, axis=-1, keepdims=True) + e_new + jnp.exp(sink - mx)
        o = _dot(e.astype(BF16), bv16) + e_new.astype(BF16).astype(F32) * vn16
        o_ref[0, kv] = (o / denom)[:, kv * HD_B:(kv + 1) * HD_B]


def _attn_b_sample(sink, qz, kn, vn, buf_k, buf_v, *, past_len):
    db, wb, kvw = buf_k.shape
    row = lambda b: (b, 0, 0)
    return pl.pallas_call(
        functools.partial(_attn_b_sample_kernel, past_len=past_len), grid=(db,),
        in_specs=[pl.BlockSpec((KV_B, G_B, 1), lambda b: (0, 0, 0)),
                  pl.BlockSpec((1, KV_B, G_B, kvw), lambda b: (b, 0, 0, 0)),
                  pl.BlockSpec((1, 1, kvw), row), pl.BlockSpec((1, 1, kvw), row),
                  pl.BlockSpec((1, wb, kvw), row), pl.BlockSpec((1, wb, kvw), row)],
        out_specs=(pl.BlockSpec((1, KV_B, G_B, HD_B), lambda b: (b, 0, 0, 0)),
                   pl.BlockSpec((1, wb, kvw), row), pl.BlockSpec((1, wb, kvw), row)),
        out_shape=(jax.ShapeDtypeStruct((db, KV_B, G_B, HD_B), F32),
                   jax.ShapeDtypeStruct((db, wb, kvw), F32), jax.ShapeDtypeStruct((db, wb, kvw), F32)),
        compiler_params=_params(("parallel",)), name="swa_sample")(sink, qz, kn, vn, buf_k, buf_v)


def _suffix_matrix():
    j = np.arange(LANES)[:, None]
    s = np.arange(LANES)[None, :]
    u = np.concatenate([(j > s).astype(np.float32), np.ones((LANES, LANES), np.float32)], axis=1)
    return jnp.asarray(np.concatenate([u, u], axis=0), dtype=BF16)


def _suffix_sums(w, u):
    hi = w.astype(BF16)
    lo = (w - hi.astype(F32)).astype(BF16)
    return _dot(jnp.concatenate([hi, lo], axis=1), u)


def _attn_c_kernel(u_ref, q_ref, k_ref, v_ref, o_ref, carry_sc, acc_sc):
    qi = pl.program_id(2)
    ki = pl.program_id(3)

    @pl.when(ki == 0)
    def _():
        carry_sc[...] = jnp.zeros_like(carry_sc)
        acc_sc[...] = jnp.zeros_like(acc_sc)

    def step(diag):
        k = k_ref[...]
        v = v_ref[...]
        u = u_ref[...]
        tq, tk = q_ref.shape[0], k.shape[0]
        if diag:
            keep = (lax.broadcasted_iota(jnp.int32, (tq, tk), 1)
                    < lax.broadcasted_iota(jnp.int32, (tq, tk), 0))
        def logits(g):
            return _dot_nt(q_ref[:, g * HD_C:(g + 1) * HD_C], k)

        z_next = logits(0)
        for g in range(G_C):
            z = z_next
            if g + 1 < G_C:
                z_next = logits(g + 1)
            lb, w = _stick_logs(z)
            if diag:
                w = jnp.where(keep, w, 0.0)
            carry = carry_sc[g]
            a_blocks = [None] * (tk // LANES)
            for sb in reversed(range(tk // LANES)):
                cols = slice(sb * LANES, (sb + 1) * LANES)
                sums = _suffix_sums(w[:, cols], u)
                a = jnp.exp2(lb[:, cols] - (carry + sums[:, :LANES]))
                if diag:
                    a = jnp.where(keep[:, cols], a, 0.0)
                a_blocks[sb] = a.astype(BF16)
                carry = carry + sums[:, LANES:]
            carry_sc[g] = carry
            acc_sc[g] += _dot(jnp.concatenate(a_blocks, axis=1), v)

    @pl.when(ki == 0)
    def _():
        step(True)

    @pl.when((ki > 0) & (ki <= qi))
    def _():
        step(False)

    @pl.when(ki == qi)
    def _():
        for g in range(G_C):
            o_ref[:, g * HD_C:(g + 1) * HD_C] = acc_sc[g].astype(o_ref.dtype)


def _attn_c_prompt(u, q, k, v, *, batch, seq, tq):
    nq = seq // tq
    gw = G_C * HD_C
    kmap = lambda b, h, i, j: (b * nq + jnp.maximum(i - j, 0), h)
    return pl.pallas_call(
        _attn_c_kernel, grid=(batch, KV_C, nq, nq),
        in_specs=[pl.BlockSpec((2 * LANES, 2 * LANES), lambda b, h, i, j: (0, 0)),
                  pl.BlockSpec((tq, gw), lambda b, h, i, j: (b * nq + i, h)),
                  pl.BlockSpec((tq, HD_C), kmap), pl.BlockSpec((tq, HD_C), kmap)],
        out_specs=pl.BlockSpec((tq, gw), lambda b, h, i, j: (b * nq + i, h)),
        out_shape=jax.ShapeDtypeStruct((batch * seq, H_C * HD_C), BF16),
        scratch_shapes=[pltpu.VMEM((G_C, tq, LANES), F32), pltpu.VMEM((G_C, tq, HD_C), F32)],
        compiler_params=_params(("parallel", "parallel", "parallel", "arbitrary")),
        name="sb_attn_prompt")(u, q, k, v)


def _attn_c_sample_kernel(pt_ref, u_ref, qz_ref, *rest, pps):
    k_refs, v_refs = rest[:pps], rest[pps:2 * pps]
    o_ref, carry_sc, acc_sc = rest[2 * pps:]
    st = pl.program_id(1)

    @pl.when(st == 0)
    def _():
        carry_sc[...] = jnp.zeros_like(carry_sc)
        acc_sc[...] = jnp.zeros_like(acc_sc)

    page = k_refs[0].shape[0]
    z = None
    for kv in range(KV_C):
        k_kv = jnp.concatenate([_page_rows(r, kv).astype(BF16) for r in k_refs], axis=0)
        zk = _dot_nt(qz_ref[0, kv], k_kv)
        z = zk if z is None else z + zk
    lb, w = _stick_logs(z)
    w_rows = jnp.concatenate([w[:, i * page:(i + 1) * page] for i in range(pps)], axis=0)
    sums = _suffix_sums(w_rows, u_ref[...])
    carry = carry_sc[...]
    a_blocks = [None] * pps
    for i in reversed(range(pps)):
        rows = slice(i * H_C, (i + 1) * H_C)
        a_blocks[i] = jnp.exp2(lb[:, i * page:(i + 1) * page] - (carry + sums[rows, :LANES]))
        carry = carry + sums[rows, LANES:]
    carry_sc[...] = carry
    a = jnp.concatenate(a_blocks, axis=1)
    row_kv = lax.broadcasted_iota(jnp.int32, a.shape, 0) // G_C
    acc = acc_sc[...]
    for kv in range(KV_C):
        v_kv = jnp.concatenate([_page_rows(r, kv).astype(BF16) for r in v_refs], axis=0)
        acc = acc + _dot(jnp.where(row_kv == kv, a, 0.0).astype(BF16), v_kv)
    acc_sc[...] = acc

    @pl.when(st == pl.num_programs(1) - 1)
    def _():
        o_ref[0] = acc


def _attn_c_sample(pt, u, qz, cache_k, cache_v, *, layer, pps):
    db = qz.shape[0]
    n_pages = pt.shape[0] // db
    n_steps = n_pages // pps
    page = cache_k.shape[2]

    def page_map(i):
        return lambda b, s, pt_ref: (layer, pt_ref[b * n_pages + (n_steps - 1 - s) * pps + i], 0, 0, 0)

    in_specs = [pl.BlockSpec((2 * LANES, 2 * LANES), lambda b, s, pt_ref: (0, 0)),
                pl.BlockSpec((1, KV_C, H_C, HD_C), lambda b, s, pt_ref: (b, 0, 0, 0))]
    in_specs += [pl.BlockSpec((None, None, page, KV_C, HD_C), page_map(i)) for i in range(pps)] * 2
    gs = pltpu.PrefetchScalarGridSpec(
        num_scalar_prefetch=1, grid=(db, n_steps), in_specs=in_specs,
        out_specs=pl.BlockSpec((1, H_C, HD_C), lambda b, s, pt_ref: (b, 0, 0)),
        scratch_shapes=[pltpu.VMEM((H_C, LANES), F32), pltpu.VMEM((H_C, HD_C), F32)])
    return pl.pallas_call(
        functools.partial(_attn_c_sample_kernel, pps=pps),
        grid_spec=gs, out_shape=jax.ShapeDtypeStruct((db, H_C, HD_C), F32),
        compiler_params=_params(("parallel", "arbitrary")), name="sb_attn_sample")(
            pt, u, qz, *([cache_k] * pps), *([cache_v] * pps))


def _pick(n, pref):
    return pref if n % pref == 0 else n


def kernel(x_prompt, x_sample, cache_a_k, cache_a_v, cache_c_k, cache_c_v, state_win_k, state_win_v, page_table, g_mix_pre, g_mix_post, g_ffn_pre, g_ffn_post, w_a_qkv, lam_q1, lam_k1, lam_q2, lam_k2, g_a_sub, w_a_out, w_b_qkv, b_sink, w_b_out, w_c_qkv, w_c_out, w_up, w_down):
    batch, seq, _ = x_prompt.shape
    db, dseq, _ = x_sample.shape
    assert dseq == 1
    depth = g_mix_pre.shape[0]
    n_pages = page_table.shape[1]
    page = cache_a_k.shape[2]
    past_len = n_pages * page
    m_p = batch * seq
    tm = _pick(m_p, 256)
    tq = _pick(seq, 512)
    tm_mlp = _pick(m_p, 512)
    pps = _pick(n_pages, 16)
    scale_a = HD_A ** -0.5 * math.log2(math.e)
    scale_c = HD_C ** -0.5 * math.log2(math.e)

    xp = x_prompt.reshape(m_p, D_MODEL)
    xs = x_sample.reshape(db, D_MODEL)
    pt = page_table.reshape(-1)
    pos_p = jnp.arange(seq, dtype=jnp.int32)
    pos_s = jnp.full((db,), past_len, dtype=jnp.int32)
    tabs = {hd: (_rope_tables(pos_p, hd), _rope_tables(pos_s, hd)) for hd in (HD_A, HD_B)}
    u = _suffix_matrix()

    w_up16, w_down16 = w_up.astype(BF16), w_down.astype(BF16)
    w_qkv16 = (w_a_qkv.astype(BF16), w_b_qkv.astype(BF16), w_c_qkv.astype(BF16))
    w_out16 = (w_a_out.astype(BF16), w_b_out.astype(BF16), w_c_out.astype(BF16))

    def token_mixers(x, o, w_out, j, i, tm_o, tm_f):
        x = _outproj(o, w_out, j, x, g_mix_post[i], tm=tm_o)
        return _mlp(x, g_ffn_pre[i], w_up16, w_down16, i, g_ffn_post[i], tm=tm_f, tf=512)

    outs = {name: [] for name in ("a_kp", "a_vp", "a_ks", "a_vs", "b_kp", "b_vp", "b_ks", "b_vs",
                                  "c_kp", "c_vp", "c_ks", "c_vs")}
    for i in range(depth):
        kind, j = i % N_MIXERS, i // N_MIXERS
        w, w_out = w_qkv16[kind], w_out16[kind]
        if kind == 0:
            lam_init = 0.8 - 0.6 * math.exp(-0.3 * i)
            lamvec = jnp.stack([lam_q1[j], lam_k1[j], lam_q2[j], lam_k2[j]])
            kw, vw = KV_A * 2 * HD_A, KV_A * VD_A
            q, k32, v32, k16, v16t = _qkv(xp, g_mix_pre[i], w, j,tabs[HD_A][0], rope_kind=HD_A, kw=kw, vw=vw, tm=tm,
                                          q_scale=scale_a, v_transposed=True)
            o = _attn_a_prompt(q, k16, v16t, lamvec, g_a_sub[j], batch=batch, seq=seq, lam_init=lam_init, tq=tq)
            qs, ks32, vs32, _, _ = _qkv(xs, g_mix_pre[i], w, j,tabs[HD_A][1], rope_kind=HD_A, kw=kw, vw=vw, tm=db,
                                        q_scale=scale_a)
            q5 = qs.reshape(db, KV_A, G_A, 2, HD_A)
            zero = jnp.zeros((db, KV_A, G_A, HD_A), BF16)
            qz = jnp.concatenate([jnp.concatenate([q5[:, :, :, 0], zero], axis=-1),
                                  jnp.concatenate([zero, q5[:, :, :, 1]], axis=-1)], axis=2)
            os_ = _attn_a_sample(pt, lamvec, g_a_sub[j], qz, ks32.reshape(db, 1, kw), vs32.reshape(db, 1, vw),
                                 cache_a_k, cache_a_v, layer=j, lam_init=lam_init, pps=pps)
            os_ = os_.reshape(db, H_A * VD_A)
            outs["a_kp"].append(k32.reshape(batch, seq, KV_A, 2, HD_A))
            outs["a_vp"].append(v32.reshape(batch, seq, KV_A, VD_A))
            outs["a_ks"].append(ks32.reshape(db, 1, KV_A, 2, HD_A))
            outs["a_vs"].append(vs32.reshape(db, 1, KV_A, VD_A))
        elif kind == 1:
            kw = vw = KV_B * HD_B
            q, k32, v32, k16, v16 = _qkv(xp, g_mix_pre[i], w, j,tabs[HD_B][0], rope_kind=HD_B, kw=kw, vw=vw, tm=tm,
                                         q_scale=1.0)
            o = _attn_b_prompt(b_sink[j], q, k16, v16, batch=batch, seq=seq)
            qs, ks32, vs32, _, _ = _qkv(xs, g_mix_pre[i], w, j,tabs[HD_B][1], rope_kind=HD_B, kw=kw, vw=vw, tm=db,
                                        q_scale=1.0)
            q4 = qs.reshape(db, KV_B, G_B, HD_B)
            qz = jnp.stack([jnp.pad(q4[:, kv], ((0, 0), (0, 0), (kv * HD_B, (KV_B - 1 - kv) * HD_B)))
                            for kv in range(KV_B)], axis=1)
            wb = state_win_k.shape[2]
            os_, nk, nv = _attn_b_sample(b_sink[j].reshape(KV_B, G_B, 1), qz, ks32.reshape(db, 1, kw),
                                         vs32.reshape(db, 1, vw), state_win_k[j].reshape(db, wb, kw),
                                         state_win_v[j].reshape(db, wb, vw), past_len=past_len)
            os_ = os_.reshape(db, H_B * HD_B)
            outs["b_kp"].append(k32.reshape(batch, seq, KV_B, HD_B)[:, seq - WINDOW:])
            outs["b_vp"].append(v32.reshape(batch, seq, KV_B, HD_B)[:, seq - WINDOW:])
            outs["b_ks"].append(nk.reshape(db, wb, KV_B, HD_B))
            outs["b_vs"].append(nv.reshape(db, wb, KV_B, HD_B))
        else:
            kw = vw = KV_C * HD_C
            q, k32, v32, k16, v16 = _qkv(xp, g_mix_pre[i], w, j,None, rope_kind=None, kw=kw, vw=vw, tm=tm,
                                         q_scale=scale_c)
            o = _attn_c_prompt(u, q, k16, v16, batch=batch, seq=seq, tq=tq)
            qs, ks32, vs32, _, _ = _qkv(xs, g_mix_pre[i], w, j,None, rope_kind=None, kw=kw, vw=vw, tm=db,
                                        q_scale=scale_c)
            q4 = qs.reshape(db, KV_C, G_C, HD_C)
            qz = jnp.stack([jnp.pad(q4[:, kv], ((0, 0), (kv * G_C, (KV_C - 1 - kv) * G_C), (0, 0)))
                            for kv in range(KV_C)], axis=1)
            os_ = _attn_c_sample(pt, u, qz, cache_c_k, cache_c_v, layer=j, pps=pps).reshape(db, H_C * HD_C)
            outs["c_kp"].append(k32.reshape(batch, seq, KV_C, HD_C))
            outs["c_vp"].append(v32.reshape(batch, seq, KV_C, HD_C))
            outs["c_ks"].append(ks32.reshape(db, 1, KV_C, HD_C))
            outs["c_vs"].append(vs32.reshape(db, 1, KV_C, HD_C))
        xp = token_mixers(xp, o, w_out, j, i, tm, tm_mlp)
        xs = token_mixers(xs, os_, w_out, j, i, db, db)
    state = [jnp.stack(outs[name]) for name in ("a_kp", "a_vp", "a_ks", "a_vs", "b_kp", "b_vp", "b_ks", "b_vs",
                                               "c_kp", "c_vp", "c_ks", "c_vs")]
    return (xp.reshape(batch, seq, D_MODEL), xs.reshape(db, 1, D_MODEL), *state)
```

```python
import functools
import math

import numpy as np
import jax
import jax.numpy as jnp
from jax import lax
from jax.experimental import pallas as pl
from jax.experimental.pallas import tpu as pltpu

D_MODEL = 2048
N_MIXERS = 3
HD_A, KV_A = 128, 2
H_A = D_MODEL // (2 * HD_A)
G_A = H_A // KV_A
VD_A = 2 * HD_A
HD_B, KV_B = 64, 4
H_B = D_MODEL // HD_B
G_B = H_B // KV_B
WINDOW = 128
HD_C, KV_C = 128, 4
H_C = D_MODEL // HD_C
G_C = H_C // KV_C
ROPE_THETA = 10000.0
EPS = 1e-6

LANES = 128
VMEM_LIMIT = 56 * 1024 * 1024
F32 = jnp.float32
BF16 = jnp.bfloat16
NEG_INF = float("-inf")
LOGITS_AHEAD = 2


def _params(sem):
    return pltpu.CompilerParams(dimension_semantics=sem, vmem_limit_bytes=VMEM_LIMIT)


def _rms(x, g):
    return x * lax.rsqrt(jnp.mean(x * x, axis=-1, keepdims=True) + EPS) * g


def _dot(a, b):
    return jnp.dot(a, b, preferred_element_type=F32)


def _dot_nt(a, b):
    return lax.dot_general(a, b, (((1,), (1,)), ((), ())), preferred_element_type=F32)


def _stick_logs(z2):
    w2 = jnp.maximum(z2, 0.0) + jnp.log2(1.0 + jnp.exp2(-jnp.abs(z2)))
    return z2 - w2, w2


def _qkv_kernel(*refs, rope_kind, kw, vw, q_scale, v_transposed):
    if rope_kind is None:
        x_ref, g_ref, w_ref, q_ref, k32_ref, v32_ref, k16_ref, v16_ref = refs
    else:
        x_ref, g_ref, w_ref, tab_ref, q_ref, k32_ref, v32_ref, k16_ref, v16_ref = refs
    xn = _rms(x_ref[...], g_ref[...]).astype(BF16)

    def rope(y):
        if rope_kind is None:
            return y
        if rope_kind == 128:
            return y * tab_ref[0] + pltpu.roll(y, 64, 1) * tab_ref[1]
        return (y * tab_ref[0] + pltpu.roll(y, 96, 1) * tab_ref[1]
                + pltpu.roll(y, 32, 1) * tab_ref[2])

    def project(c0, width, store):
        step = min(width, 4 * LANES)
        for s0 in range(0, width, step):
            y = _dot(xn, w_ref[:, c0 + s0:c0 + s0 + step])
            for t in range(step // LANES):
                store(s0 + t * LANES, y[:, t * LANES:(t + 1) * LANES])

    def store_q(c, y):
        y = rope(y)
        if q_scale != 1.0:
            y = y * q_scale
        q_ref[:, c:c + LANES] = y.astype(BF16)

    def store_k(c, y):
        y = rope(y)
        k32_ref[:, c:c + LANES] = y
        k16_ref[:, c:c + LANES] = y.astype(BF16)

    def store_v(c, y):
        v32_ref[:, c:c + LANES] = y
        if v_transposed:
            v16_ref[c:c + LANES, :] = y.T.astype(BF16)
        else:
            v16_ref[:, c:c + LANES] = y.astype(BF16)

    project(0, D_MODEL, store_q)
    project(D_MODEL, kw, store_k)
    project(D_MODEL + kw, vw, store_v)


def _qkv(x, g, w, layer, tab, *, rope_kind, kw, vw, tm, q_scale, v_transposed=False):
    m = x.shape[0]
    n = w.shape[2]
    grid = (m // tm,)
    in_specs = [pl.BlockSpec((tm, D_MODEL), lambda i: (i, 0)),
                pl.BlockSpec((1, D_MODEL), lambda i: (0, 0)),
                pl.BlockSpec((None, D_MODEL, n), lambda i: (layer, 0, 0))]
    args = [x, g.reshape(1, D_MODEL), w]
    if rope_kind is not None:
        nt = tab.shape[1] // tm
        in_specs.append(pl.BlockSpec((3, tm, LANES), lambda i: (0, i % nt, 0)))
        args.append(tab)
    if v_transposed:
        v16_shape, v16_spec = (vw, m), pl.BlockSpec((vw, tm), lambda i: (0, i))
    else:
        v16_shape, v16_spec = (m, vw), pl.BlockSpec((tm, vw), lambda i: (i, 0))
    out_shape = (jax.ShapeDtypeStruct((m, D_MODEL), BF16),
                 jax.ShapeDtypeStruct((m, kw), F32), jax.ShapeDtypeStruct((m, vw), F32),
                 jax.ShapeDtypeStruct((m, kw), BF16), jax.ShapeDtypeStruct(v16_shape, BF16))
    out_specs = (pl.BlockSpec((tm, D_MODEL), lambda i: (i, 0)),
                 pl.BlockSpec((tm, kw), lambda i: (i, 0)), pl.BlockSpec((tm, vw), lambda i: (i, 0)),
                 pl.BlockSpec((tm, kw), lambda i: (i, 0)), v16_spec)
    return pl.pallas_call(
        functools.partial(_qkv_kernel, rope_kind=rope_kind, kw=kw, vw=vw, q_scale=q_scale,
                          v_transposed=v_transposed),
        grid=grid, in_specs=in_specs, out_specs=out_specs, out_shape=out_shape,
        compiler_params=_params(("parallel",)), name="qkv_proj")(*args)


def _rope_tables(pos, hd):
    half = hd // 2
    inv_freq = ROPE_THETA ** (-jnp.arange(half, dtype=F32) * 2.0 / hd)
    ang = pos.astype(F32)[:, None] * inv_freq[None, :]
    cos, sin = jnp.cos(ang), jnp.sin(ang)
    zero = jnp.zeros_like(sin)
    reps = LANES // hd
    cos_t = jnp.tile(jnp.concatenate([cos, cos], axis=-1), (1, reps))
    if hd == LANES:
        s1 = jnp.concatenate([-sin, sin], axis=-1)
        s2 = jnp.zeros_like(s1)
    else:
        s1 = jnp.tile(jnp.concatenate([-sin, zero], axis=-1), (1, reps))
        s2 = jnp.tile(jnp.concatenate([zero, sin], axis=-1), (1, reps))
    return jnp.stack([cos_t, s1, s2])


def _outproj_kernel(o_ref, w_ref, x_ref, g_ref, y_ref):
    y = _dot(o_ref[...].astype(BF16), w_ref[...])
    y_ref[...] = x_ref[...] + _rms(y, g_ref[...])


def _outproj(o, w, layer, x, g, *, tm):
    m = x.shape[0]
    return pl.pallas_call(
        _outproj_kernel, grid=(m // tm,),
        in_specs=[pl.BlockSpec((tm, D_MODEL), lambda i: (i, 0)),
                  pl.BlockSpec((None, D_MODEL, D_MODEL), lambda i: (layer, 0, 0)),
                  pl.BlockSpec((tm, D_MODEL), lambda i: (i, 0)),
                  pl.BlockSpec((1, D_MODEL), lambda i: (0, 0))],
        out_specs=pl.BlockSpec((tm, D_MODEL), lambda i: (i, 0)),
        out_shape=jax.ShapeDtypeStruct((m, D_MODEL), F32),
        compiler_params=_params(("parallel",)), name="out_proj")(o, w, x, g.reshape(1, D_MODEL))


def _mlp_kernel(x_ref, gpre_ref, wup_ref, wdn_ref, gpost_ref, y_ref, xn_sc, acc_sc):
    f = pl.program_id(1)

    @pl.when(f == 0)
    def _():
        xn_sc[...] = _rms(x_ref[...], gpre_ref[...]).astype(BF16)
        acc_sc[...] = jnp.zeros_like(acc_sc)

    h = jnp.maximum(_dot(xn_sc[...], wup_ref[...]), 0.0)
    acc_sc[...] += _dot((h * h).astype(BF16), wdn_ref[...])

    @pl.when(f == pl.num_programs(1) - 1)
    def _():
        y_ref[...] = x_ref[...] + _rms(acc_sc[...], gpost_ref[...])


def _mlp(x, gpre, wup, wdn, layer, gpost, *, tm, tf):
    m = x.shape[0]
    d_ff = wup.shape[2]
    return pl.pallas_call(
        _mlp_kernel, grid=(m // tm, d_ff // tf),
        in_specs=[pl.BlockSpec((tm, D_MODEL), lambda i, f: (i, 0)),
                  pl.BlockSpec((1, D_MODEL), lambda i, f: (0, 0)),
                  pl.BlockSpec((None, D_MODEL, tf), lambda i, f: (layer, 0, f)),
                  pl.BlockSpec((None, tf, D_MODEL), lambda i, f: (layer, f, 0)),
                  pl.BlockSpec((1, D_MODEL), lambda i, f: (0, 0))],
        out_specs=pl.BlockSpec((tm, D_MODEL), lambda i, f: (i, 0)),
        out_shape=jax.ShapeDtypeStruct((m, D_MODEL), F32),
        scratch_shapes=[pltpu.VMEM((tm, D_MODEL), BF16), pltpu.VMEM((tm, D_MODEL), F32)],
        compiler_params=_params(("parallel", "arbitrary")), name="mlp")(
            x, gpre.reshape(1, D_MODEL), wup, wdn, gpost.reshape(1, D_MODEL))


def _mlp_cast_kernel(x_ref, gpre_ref, wup32_ref, wdn32_ref, gpost_ref, y_ref, wup16_ref, wdn16_ref,
                     xn_sc, acc_sc):
    wup16_ref[...] = wup32_ref[...].astype(BF16)
    wdn16_ref[...] = wdn32_ref[...].astype(BF16)
    _mlp_kernel(x_ref, gpre_ref, wup16_ref, wdn16_ref, gpost_ref, y_ref, xn_sc, acc_sc)


def _mlp_cast(x, gpre, wup32, wdn32, layer, gpost, *, tf):
    m = x.shape[0]
    d_ff = wup32.shape[2]
    return pl.pallas_call(
        _mlp_cast_kernel, grid=(1, d_ff // tf),
        in_specs=[pl.BlockSpec((m, D_MODEL), lambda i, f: (0, 0)),
                  pl.BlockSpec((1, D_MODEL), lambda i, f: (0, 0)),
                  pl.BlockSpec((None, D_MODEL, tf), lambda i, f: (layer, 0, f)),
                  pl.BlockSpec((None, tf, D_MODEL), lambda i, f: (layer, f, 0)),
                  pl.BlockSpec((1, D_MODEL), lambda i, f: (0, 0))],
        out_specs=(pl.BlockSpec((m, D_MODEL), lambda i, f: (0, 0)),
                   pl.BlockSpec((None, D_MODEL, tf), lambda i, f: (0, 0, f)),
                   pl.BlockSpec((None, tf, D_MODEL), lambda i, f: (0, f, 0))),
        out_shape=(jax.ShapeDtypeStruct((m, D_MODEL), F32),
                   jax.ShapeDtypeStruct((1, D_MODEL, d_ff), BF16),
                   jax.ShapeDtypeStruct((1, d_ff, D_MODEL), BF16)),
        scratch_shapes=[pltpu.VMEM((m, D_MODEL), BF16), pltpu.VMEM((m, D_MODEL), F32)],
        compiler_params=_params(("arbitrary", "arbitrary")), name="mlp_cast")(
            x, gpre.reshape(1, D_MODEL), wup32, wdn32, gpost.reshape(1, D_MODEL))


def _diff_lambda(lam_ref, lam_init):
    d1 = jnp.sum(lam_ref[0:1, :] * lam_ref[1:2, :], axis=-1, keepdims=True)
    d2 = jnp.sum(lam_ref[2:3, :] * lam_ref[3:4, :], axis=-1, keepdims=True)
    return jnp.exp(d1) - jnp.exp(d2) + lam_init


def _causal_steps(nq, newest_first):
    pairs = [(i, j) for i in range(nq) for j in (range(i, -1, -1) if newest_first else range(i + 1))]
    qt, kt = zip(*pairs)
    return jnp.asarray(qt, jnp.int32), jnp.asarray(kt, jnp.int32)


def _attn_a_kernel(qt_ref, kt_ref, lam_ref, gsub_ref, q_ref, k_ref, vt_ref, o_ref, m_sc, l_sc, acc_sc,
                   *, lam_init):
    qi = qt_ref[pl.program_id(2)]
    ki = kt_ref[pl.program_id(2)]

    @pl.when(ki == 0)
    def _():
        m_sc[...] = jnp.full_like(m_sc, NEG_INF)
        l_sc[...] = jnp.zeros_like(l_sc)
        acc_sc[...] = jnp.zeros_like(acc_sc)

    def step(diag):
        vt = vt_ref[...]
        tq, tk = q_ref.shape[0], k_ref.shape[0]
        if diag:
            keep = (lax.broadcasted_iota(jnp.int32, (tk, tq), 0)
                    <= lax.broadcasted_iota(jnp.int32, (tk, tq), 1))
        def logits(idx):
            c = idx % 2
            return _dot_nt(k_ref[:, c * HD_A:(c + 1) * HD_A], q_ref[:, idx * HD_A:(idx + 1) * HD_A])

        queue = [logits(idx) for idx in range(LOGITS_AHEAD)]
        for idx in range(2 * G_A):
            s = queue.pop(0)
            if idx + LOGITS_AHEAD < 2 * G_A:
                queue.append(logits(idx + LOGITS_AHEAD))
            if diag:
                s = jnp.where(keep, s, NEG_INF)
            m_prev = m_sc[idx]
            m_new = jnp.maximum(m_prev, jnp.max(s, axis=0, keepdims=True))
            alpha = jnp.exp2(m_prev - m_new)
            p = jnp.exp2(s - m_new)
            l_sc[idx] = alpha * l_sc[idx] + jnp.sum(p, axis=0, keepdims=True)
            acc_sc[idx] = alpha * acc_sc[idx] + _dot(vt, p.astype(BF16))
            m_sc[idx] = m_new

    @pl.when(ki < qi)
    def _():
        step(False)

    @pl.when(ki == qi)
    def _():
        step(True)
        lam = _diff_lambda(lam_ref, lam_init)
        for g in range(G_A):
            ot = acc_sc[2 * g] / l_sc[2 * g] - lam * (acc_sc[2 * g + 1] / l_sc[2 * g + 1])
            o = _rms(ot.T, gsub_ref[...]) * (1.0 - lam_init)
            o_ref[:, g * VD_A:(g + 1) * VD_A] = o.astype(o_ref.dtype)


def _attn_a_prompt(q, k, vt, lamvec, gsub, *, batch, seq, lam_init, tq):
    nq = seq // tq
    gw = G_A * VD_A
    qt, kt = _causal_steps(nq, newest_first=False)
    gs = pltpu.PrefetchScalarGridSpec(
        num_scalar_prefetch=2, grid=(batch, KV_A, qt.shape[0]),
        in_specs=[pl.BlockSpec((4, HD_A), lambda b, h, t, qt, kt: (0, 0)),
                  pl.BlockSpec((1, VD_A), lambda b, h, t, qt, kt: (0, 0)),
                  pl.BlockSpec((tq, gw), lambda b, h, t, qt, kt: (b * nq + qt[t], h)),
                  pl.BlockSpec((tq, 2 * HD_A), lambda b, h, t, qt, kt: (b * nq + kt[t], h)),
                  pl.BlockSpec((VD_A, tq), lambda b, h, t, qt, kt: (h, b * nq + kt[t]))],
        out_specs=pl.BlockSpec((tq, gw), lambda b, h, t, qt, kt: (b * nq + qt[t], h)),
        scratch_shapes=[pltpu.VMEM((2 * G_A, 1, tq), F32), pltpu.VMEM((2 * G_A, 1, tq), F32),
                        pltpu.VMEM((2 * G_A, VD_A, tq), F32)])
    return pl.pallas_call(
        functools.partial(_attn_a_kernel, lam_init=lam_init), grid_spec=gs,
        out_shape=jax.ShapeDtypeStruct((batch * seq, H_A * VD_A), BF16),
        compiler_params=_params(("parallel", "parallel", "arbitrary")),
        name="diff_attn_prompt")(qt, kt, lamvec, gsub.reshape(1, VD_A), q, k, vt)


def _page_rows(ref, offset):
    tokens = ref.shape[0]
    stride = math.prod(ref.shape[1:-1])
    flat = ref.reshape(tokens * stride, ref.shape[-1])
    return flat[pl.ds(offset, tokens, stride=stride), :]


def _attn_a_sample_kernel(pt_ref, lam_ref, gsub_ref, qz_ref, kn_ref, vn_ref, *rest, pps, lam_init):
    k_refs, v_refs = rest[:pps], rest[pps:2 * pps]
    o_ref, m_sc, l_sc, acc_sc = rest[2 * pps:]
    st = pl.program_id(1)

    @pl.when(st == 0)
    def _():
        for kv in range(KV_A):
            kn = kn_ref[0, :, kv * 2 * HD_A:(kv + 1) * 2 * HD_A].astype(BF16).astype(F32)
            vn = vn_ref[0, :, kv * VD_A:(kv + 1) * VD_A].astype(BF16).astype(F32)
            s = jnp.sum(qz_ref[0, kv].astype(F32) * kn, axis=-1, keepdims=True)
            m_sc[kv] = s
            l_sc[kv] = jnp.ones_like(s)
            acc_sc[kv] = jnp.broadcast_to(vn, acc_sc.shape[1:])

    for kv in range(KV_A):
        k_kv = jnp.concatenate(
            [jnp.concatenate([_page_rows(r, 2 * kv + c).astype(BF16) for c in range(2)], axis=1)
             for r in k_refs], axis=0)
        v_kv = jnp.concatenate(
            [jnp.concatenate([_page_rows(r, h * KV_A + kv).astype(BF16) for h in range(VD_A // LANES)], axis=1)
             for r in v_refs], axis=0)
        s = _dot_nt(qz_ref[0, kv], k_kv)
        m_prev = m_sc[kv]
        m_new = jnp.maximum(m_prev, jnp.max(s, axis=-1, keepdims=True))
        alpha = jnp.exp2(m_prev - m_new)
        p = jnp.exp2(s - m_new)
        l_sc[kv] = alpha * l_sc[kv] + jnp.sum(p, axis=-1, keepdims=True)
        acc_sc[kv] = alpha * acc_sc[kv] + _dot(p.astype(BF16), v_kv)
        m_sc[kv] = m_new

    @pl.when(st == pl.num_programs(1) - 1)
    def _():
        lam = _diff_lambda(lam_ref, lam_init)
        for kv in range(KV_A):
            on = acc_sc[kv] / l_sc[kv]
            o = on[0:G_A] - lam * on[G_A:2 * G_A]
            o_ref[0, kv] = _rms(o, gsub_ref[...]) * (1.0 - lam_init)


def _attn_a_sample(pt, lamvec, gsub, qz, kn, vn, cache_k, cache_v, *, layer, lam_init, pps):
    db = qz.shape[0]
    n_pages = pt.shape[0] // db
    page = cache_k.shape[2]

    def page_map(i, tail):
        return lambda b, s, pt_ref: (layer, pt_ref[b * n_pages + s * pps + i]) + tail

    in_specs = [pl.BlockSpec((4, HD_A), lambda b, s, pt_ref: (0, 0)),
                pl.BlockSpec((1, VD_A), lambda b, s, pt_ref: (0, 0)),
                pl.BlockSpec((1, KV_A, 2 * G_A, 2 * HD_A), lambda b, s, pt_ref: (b, 0, 0, 0)),
                pl.BlockSpec((1, 1, KV_A * 2 * HD_A), lambda b, s, pt_ref: (b, 0, 0)),
                pl.BlockSpec((1, 1, KV_A * VD_A), lambda b, s, pt_ref: (b, 0, 0))]
    in_specs += [pl.BlockSpec((None, None, page, KV_A, 2, HD_A), page_map(i, (0, 0, 0, 0)))
                 for i in range(pps)]
    in_specs += [pl.BlockSpec((None, None, page, VD_A // LANES, KV_A, LANES), page_map(i, (0, 0, 0, 0)))
                 for i in range(pps)]
    gs = pltpu.PrefetchScalarGridSpec(
        num_scalar_prefetch=1, grid=(db, n_pages // pps), in_specs=in_specs,
        out_specs=pl.BlockSpec((1, KV_A, G_A, VD_A), lambda b, s, pt_ref: (b, 0, 0, 0)),
        scratch_shapes=[pltpu.VMEM((KV_A, 2 * G_A, 1), F32), pltpu.VMEM((KV_A, 2 * G_A, 1), F32),
                        pltpu.VMEM((KV_A, 2 * G_A, VD_A), F32)])
    return pl.pallas_call(
        functools.partial(_attn_a_sample_kernel, pps=pps, lam_init=lam_init),
        grid_spec=gs, out_shape=jax.ShapeDtypeStruct((db, KV_A, G_A, VD_A), F32),
        compiler_params=_params(("parallel", "arbitrary")), name="diff_attn_sample")(
            pt, lamvec, gsub.reshape(1, VD_A), qz, kn, vn,
            *([cache_k] * pps), *([cache_v] * pps))


def _attn_b_kernel(sink_ref, q_ref, kc_ref, kp_ref, vc_ref, vp_ref, o_ref):
    i = pl.program_id(1)
    w = q_ref.shape[0]
    scale = HD_B ** -0.5
    kk = jnp.concatenate([kp_ref[...], kc_ref[...]], axis=0)
    vv = jnp.concatenate([vp_ref[...], vc_ref[...]], axis=0)
    row = lax.broadcasted_iota(jnp.int32, (w, 2 * w), 0)
    col = lax.broadcasted_iota(jnp.int32, (w, 2 * w), 1)
    keep = (col >= row) & (col <= row + WINDOW) & ((i > 0) | (col >= w))
    for kv in range(KV_B):
        k = kk[:, kv * HD_B:(kv + 1) * HD_B]
        v = vv[:, kv * HD_B:(kv + 1) * HD_B]
        outs = []
        for g in range(G_B):
            h = kv * G_B + g
            q = q_ref[:, h * HD_B:(h + 1) * HD_B]
            s = jnp.where(keep, _dot_nt(q, k) * scale, NEG_INF)
            sink = sink_ref[h]
            mx = jnp.maximum(jnp.max(s, axis=-1, keepdims=True), sink)
            e = jnp.exp(s - mx)
            denom = jnp.sum(e, axis=-1, keepdims=True) + jnp.exp(sink - mx)
            outs.append(_dot(e.astype(BF16), v) / denom)
        for t in range(G_B // 2):
            h0 = kv * G_B + 2 * t
            o_ref[:, h0 * HD_B:(h0 + 2) * HD_B] = jnp.concatenate(
                [outs[2 * t], outs[2 * t + 1]], axis=1).astype(o_ref.dtype)


def _attn_b_prompt(sink, q, k, v, *, batch, seq):
    nb = seq // WINDOW
    kvw = KV_B * HD_B
    cur = lambda b, i: (b * nb + i, 0)
    prev = lambda b, i: (b * nb + jnp.maximum(i - 1, 0), 0)
    return pl.pallas_call(
        _attn_b_kernel, grid=(batch, nb),
        in_specs=[pl.BlockSpec(memory_space=pltpu.SMEM),
                  pl.BlockSpec((WINDOW, D_MODEL), cur),
                  pl.BlockSpec((WINDOW, kvw), cur), pl.BlockSpec((WINDOW, kvw), prev),
                  pl.BlockSpec((WINDOW, kvw), cur), pl.BlockSpec((WINDOW, kvw), prev)],
        out_specs=pl.BlockSpec((WINDOW, D_MODEL), cur),
        out_shape=jax.ShapeDtypeStruct((batch * seq, D_MODEL), BF16),
        compiler_params=_params(("parallel", "parallel")), name="swa_prompt")(sink, q, k, k, v, v)


def _attn_b_sample_kernel(sink_ref, qz_ref, kn_ref, vn_ref, bk_ref, bv_ref, o_ref, nk_ref, nv_ref,
                          *, past_len):
    wb = bk_ref.shape[1]
    scale = HD_B ** -0.5
    bk = bk_ref[0]
    bv = bv_ref[0]
    kn = kn_ref[0]
    vn = vn_ref[0]
    nk_ref[0, 0:wb - 1, :] = bk[1:wb, :]
    nk_ref[0, wb - 1:wb, :] = kn
    nv_ref[0, 0:wb - 1, :] = bv[1:wb, :]
    nv_ref[0, wb - 1:wb, :] = vn
    dist = wb - lax.broadcasted_iota(jnp.int32, (G_B, wb), 1)
    keep = (dist >= 0) & (dist <= WINDOW) & (past_len - dist >= 0)
    bk16 = bk.astype(BF16)
    bv16 = bv.astype(BF16)
    kn16 = kn.astype(BF16).astype(F32)
    vn16 = vn.astype(BF16).astype(F32)
    for kv in range(KV_B):
        qz = qz_ref[0, kv]
        s = jnp.where(keep, _dot_nt(qz, bk16) * scale, NEG_INF)
        s_new = jnp.sum(qz.astype(F32) * kn16, axis=-1, keepdims=True) * scale
        sink = sink_ref[kv]
        mx = jnp.maximum(jnp.maximum(jnp.max(s, axis=-1, keepdims=True), s_new), sink)
        e = jnp.exp(s - mx)
        e_new = jnp.exp(s_new - mx)
        denom = jnp.sum(e, axis=-1, keepdims=True) + e_new + jnp.exp(sink - mx)
        o = _dot(e.astype(BF16), bv16) + e_new.astype(BF16).astype(F32) * vn16
        o_ref[0, kv] = (o / denom)[:, kv * HD_B:(kv + 1) * HD_B]


def _attn_b_sample(sink, qz, kn, vn, buf_k, buf_v, *, past_len):
    db, wb, kvw = buf_k.shape
    row = lambda b: (b, 0, 0)
    return pl.pallas_call(
        functools.partial(_attn_b_sample_kernel, past_len=past_len), grid=(db,),
        in_specs=[pl.BlockSpec((KV_B, G_B, 1), lambda b: (0, 0, 0)),
                  pl.BlockSpec((1, KV_B, G_B, kvw), lambda b: (b, 0, 0, 0)),
                  pl.BlockSpec((1, 1, kvw), row), pl.BlockSpec((1, 1, kvw), row),
                  pl.BlockSpec((1, wb, kvw), row), pl.BlockSpec((1, wb, kvw), row)],
        out_specs=(pl.BlockSpec((1, KV_B, G_B, HD_B), lambda b: (b, 0, 0, 0)),
                   pl.BlockSpec((1, wb, kvw), row), pl.BlockSpec((1, wb, kvw), row)),
        out_shape=(jax.ShapeDtypeStruct((db, KV_B, G_B, HD_B), F32),
                   jax.ShapeDtypeStruct((db, wb, kvw), F32), jax.ShapeDtypeStruct((db, wb, kvw), F32)),
        compiler_params=_params(("parallel",)), name="swa_sample")(sink, qz, kn, vn, buf_k, buf_v)


def _suffix_matrix():
    j = np.arange(LANES)[:, None]
    s = np.arange(LANES)[None, :]
    u = np.concatenate([(j > s).astype(np.float32), np.ones((LANES, LANES), np.float32)], axis=1)
    return jnp.asarray(np.concatenate([u, u], axis=0), dtype=BF16)


def _suffix_sums(w, u):
    hi = w.astype(BF16)
    lo = (w - hi.astype(F32)).astype(BF16)
    return _dot(jnp.concatenate([hi, lo], axis=1), u)


def _attn_c_kernel(qt_ref, kt_ref, u_ref, q_ref, k_ref, v_ref, o_ref, carry_sc, acc_sc):
    qi = qt_ref[pl.program_id(2)]
    kb = kt_ref[pl.program_id(2)]

    @pl.when(kb == qi)
    def _():
        carry_sc[...] = jnp.zeros_like(carry_sc)
        acc_sc[...] = jnp.zeros_like(acc_sc)

    def step(diag):
        k = k_ref[...]
        v = v_ref[...]
        u = u_ref[...]
        tq, tk = q_ref.shape[0], k.shape[0]
        if diag:
            keep = (lax.broadcasted_iota(jnp.int32, (tq, tk), 1)
                    < lax.broadcasted_iota(jnp.int32, (tq, tk), 0))
        def logits(g):
            return _dot_nt(q_ref[:, g * HD_C:(g + 1) * HD_C], k)

        z_next = logits(0)
        for g in range(G_C):
            z = z_next
            if g + 1 < G_C:
                z_next = logits(g + 1)
            lb, w = _stick_logs(z)
            if diag:
                w = jnp.where(keep, w, 0.0)
            carry = carry_sc[g]
            a_blocks = [None] * (tk // LANES)
            for sb in reversed(range(tk // LANES)):
                cols = slice(sb * LANES, (sb + 1) * LANES)
                sums = _suffix_sums(w[:, cols], u)
                a = jnp.exp2(lb[:, cols] - (carry + sums[:, :LANES]))
                if diag:
                    a = jnp.where(keep[:, cols], a, 0.0)
                a_blocks[sb] = a.astype(BF16)
                carry = carry + sums[:, LANES:]
            carry_sc[g] = carry
            acc_sc[g] += _dot(jnp.concatenate(a_blocks, axis=1), v)

    @pl.when(kb == qi)
    def _():
        step(True)

    @pl.when(kb < qi)
    def _():
        step(False)

    @pl.when(kb == 0)
    def _():
        for g in range(G_C):
            o_ref[:, g * HD_C:(g + 1) * HD_C] = acc_sc[g].astype(o_ref.dtype)


def _attn_c_prompt(u, q, k, v, *, batch, seq, tq):
    nq = seq // tq
    gw = G_C * HD_C
    qt, kt = _causal_steps(nq, newest_first=True)
    kmap = lambda b, h, t, qt, kt: (b * nq + kt[t], h)
    gs = pltpu.PrefetchScalarGridSpec(
        num_scalar_prefetch=2, grid=(batch, KV_C, qt.shape[0]),
        in_specs=[pl.BlockSpec((2 * LANES, 2 * LANES), lambda b, h, t, qt, kt: (0, 0)),
                  pl.BlockSpec((tq, gw), lambda b, h, t, qt, kt: (b * nq + qt[t], h)),
                  pl.BlockSpec((tq, HD_C), kmap), pl.BlockSpec((tq, HD_C), kmap)],
        out_specs=pl.BlockSpec((tq, gw), lambda b, h, t, qt, kt: (b * nq + qt[t], h)),
        scratch_shapes=[pltpu.VMEM((G_C, tq, LANES), F32), pltpu.VMEM((G_C, tq, HD_C), F32)])
    return pl.pallas_call(
        _attn_c_kernel, grid_spec=gs,
        out_shape=jax.ShapeDtypeStruct((batch * seq, H_C * HD_C), BF16),
        compiler_params=_params(("parallel", "parallel", "arbitrary")),
        name="sb_attn_prompt")(qt, kt, u, q, k, v)


def _attn_c_sample_kernel(pt_ref, u_ref, qz_ref, *rest, pps):
    k_refs, v_refs = rest[:pps], rest[pps:2 * pps]
    o_ref, carry_sc, acc_sc = rest[2 * pps:]
    st = pl.program_id(1)

    @pl.when(st == 0)
    def _():
        carry_sc[...] = jnp.zeros_like(carry_sc)
        acc_sc[...] = jnp.zeros_like(acc_sc)

    page = k_refs[0].shape[0]
    z = None
    for kv in range(KV_C):
        k_kv = jnp.concatenate([_page_rows(r, kv).astype(BF16) for r in k_refs], axis=0)
        zk = _dot_nt(qz_ref[0, kv], k_kv)
        z = zk if z is None else z + zk
    lb, w = _stick_logs(z)
    w_rows = jnp.concatenate([w[:, i * page:(i + 1) * page] for i in range(pps)], axis=0)
    sums = _suffix_sums(w_rows, u_ref[...])
    carry = carry_sc[...]
    a_blocks = [None] * pps
    for i in reversed(range(pps)):
        rows = slice(i * H_C, (i + 1) * H_C)
        a_blocks[i] = jnp.exp2(lb[:, i * page:(i + 1) * page] - (carry + sums[rows, :LANES]))
        carry = carry + sums[rows, LANES:]
    carry_sc[...] = carry
    a = jnp.concatenate(a_blocks, axis=1)
    row_kv = lax.broadcasted_iota(jnp.int32, a.shape, 0) // G_C
    acc = acc_sc[...]
    for kv in range(KV_C):
        v_kv = jnp.concatenate([_page_rows(r, kv).astype(BF16) for r in v_refs], axis=0)
        acc = acc + _dot(jnp.where(row_kv == kv, a, 0.0).astype(BF16), v_kv)
    acc_sc[...] = acc

    @pl.when(st == pl.num_programs(1) - 1)
    def _():
        o_ref[0] = acc


def _attn_c_sample(pt, u, qz, cache_k, cache_v, *, layer, pps):
    db = qz.shape[0]
    n_pages = pt.shape[0] // db
    n_steps = n_pages // pps
    page = cache_k.shape[2]

    def page_map(i):
        return lambda b, s, pt_ref: (layer, pt_ref[b * n_pages + (n_steps - 1 - s) * pps + i], 0, 0, 0)

    in_specs = [pl.BlockSpec((2 * LANES, 2 * LANES), lambda b, s, pt_ref: (0, 0)),
                pl.BlockSpec((1, KV_C, H_C, HD_C), lambda b, s, pt_ref: (b, 0, 0, 0))]
    in_specs += [pl.BlockSpec((None, None, page, KV_C, HD_C), page_map(i)) for i in range(pps)] * 2
    gs = pltpu.PrefetchScalarGridSpec(
        num_scalar_prefetch=1, grid=(db, n_steps), in_specs=in_specs,
        out_specs=pl.BlockSpec((1, H_C, HD_C), lambda b, s, pt_ref: (b, 0, 0)),
        scratch_shapes=[pltpu.VMEM((H_C, LANES), F32), pltpu.VMEM((H_C, HD_C), F32)])
    return pl.pallas_call(
        functools.partial(_attn_c_sample_kernel, pps=pps),
        grid_spec=gs, out_shape=jax.ShapeDtypeStruct((db, H_C, HD_C), F32),
        compiler_params=_params(("parallel", "arbitrary")), name="sb_attn_sample")(
            pt, u, qz, *([cache_k] * pps), *([cache_v] * pps))


def _pick(n, pref):
    return pref if n % pref == 0 else n


def kernel(x_prompt, x_sample, cache_a_k, cache_a_v, cache_c_k, cache_c_v, state_win_k, state_win_v, page_table, g_mix_pre, g_mix_post, g_ffn_pre, g_ffn_post, w_a_qkv, lam_q1, lam_k1, lam_q2, lam_k2, g_a_sub, w_a_out, w_b_qkv, b_sink, w_b_out, w_c_qkv, w_c_out, w_up, w_down):
    batch, seq, _ = x_prompt.shape
    db, dseq, _ = x_sample.shape
    assert dseq == 1
    depth = g_mix_pre.shape[0]
    n_pages = page_table.shape[1]
    page = cache_a_k.shape[2]
    past_len = n_pages * page
    m_p = batch * seq
    tm = _pick(m_p, 256)
    tq = _pick(seq, 512)
    tm_mlp = _pick(m_p, 512)
    pps = _pick(n_pages, 16)
    scale_a = HD_A ** -0.5 * math.log2(math.e)
    scale_c = HD_C ** -0.5 * math.log2(math.e)

    xp = x_prompt.reshape(m_p, D_MODEL)
    xs = x_sample.reshape(db, D_MODEL)
    pt = page_table.reshape(-1)
    pos_p = jnp.arange(seq, dtype=jnp.int32)
    pos_s = jnp.full((db,), past_len, dtype=jnp.int32)
    tabs = {hd: (_rope_tables(pos_p, hd), _rope_tables(pos_s, hd)) for hd in (HD_A, HD_B)}
    u = _suffix_matrix()
    cache_a_v6 = cache_a_v.reshape(*cache_a_v.shape[:3], KV_A, VD_A // LANES, LANES).transpose(0, 1, 2, 4, 3, 5)

    w_qkv16 = (w_a_qkv.astype(BF16), w_b_qkv.astype(BF16), w_c_qkv.astype(BF16))
    w_out16 = (w_a_out.astype(BF16), w_b_out.astype(BF16), w_c_out.astype(BF16))

    def token_mixers(xp, xs, o, os_, w_out, j, i):
        xs = _outproj(os_, w_out, j, xs, g_mix_post[i], tm=db)
        xs, w_up16, w_down16 = _mlp_cast(xs, g_ffn_pre[i], w_up, w_down, i, g_ffn_post[i], tf=512)
        xp = _outproj(o, w_out, j, xp, g_mix_post[i], tm=tm)
        xp = _mlp(xp, g_ffn_pre[i], w_up16, w_down16, 0, g_ffn_post[i], tm=tm_mlp, tf=512)
        return xp, xs

    outs = {name: [] for name in ("a_kp", "a_vp", "a_ks", "a_vs", "b_kp", "b_vp", "b_ks", "b_vs",
                                  "c_kp", "c_vp", "c_ks", "c_vs")}
    for i in range(depth):
        kind, j = i % N_MIXERS, i // N_MIXERS
        w, w_out = w_qkv16[kind], w_out16[kind]
        if kind == 0:
            lam_init = 0.8 - 0.6 * math.exp(-0.3 * i)
            lamvec = jnp.stack([lam_q1[j], lam_k1[j], lam_q2[j], lam_k2[j]])
            kw, vw = KV_A * 2 * HD_A, KV_A * VD_A
            q, k32, v32, k16, v16t = _qkv(xp, g_mix_pre[i], w, j,tabs[HD_A][0], rope_kind=HD_A, kw=kw, vw=vw, tm=tm,
                                          q_scale=scale_a, v_transposed=True)
            o = _attn_a_prompt(q, k16, v16t, lamvec, g_a_sub[j], batch=batch, seq=seq, lam_init=lam_init, tq=tq)
            qs, ks32, vs32, _, _ = _qkv(xs, g_mix_pre[i], w, j,tabs[HD_A][1], rope_kind=HD_A, kw=kw, vw=vw, tm=db,
                                        q_scale=scale_a)
            q5 = qs.reshape(db, KV_A, G_A, 2, HD_A)
            zero = jnp.zeros((db, KV_A, G_A, HD_A), BF16)
            qz = jnp.concatenate([jnp.concatenate([q5[:, :, :, 0], zero], axis=-1),
                                  jnp.concatenate([zero, q5[:, :, :, 1]], axis=-1)], axis=2)
            os_ = _attn_a_sample(pt, lamvec, g_a_sub[j], qz, ks32.reshape(db, 1, kw), vs32.reshape(db, 1, vw),
                                 cache_a_k, cache_a_v6, layer=j, lam_init=lam_init, pps=pps)
            os_ = os_.reshape(db, H_A * VD_A)
            outs["a_kp"].append(k32.reshape(batch, seq, KV_A, 2, HD_A))
            outs["a_vp"].append(v32.reshape(batch, seq, KV_A, VD_A))
            outs["a_ks"].append(ks32.reshape(db, 1, KV_A, 2, HD_A))
            outs["a_vs"].append(vs32.reshape(db, 1, KV_A, VD_A))
        elif kind == 1:
            kw = vw = KV_B * HD_B
            q, k32, v32, k16, v16 = _qkv(xp, g_mix_pre[i], w, j,tabs[HD_B][0], rope_kind=HD_B, kw=kw, vw=vw, tm=tm,
                                         q_scale=1.0)
            o = _attn_b_prompt(b_sink[j], q, k16, v16, batch=batch, seq=seq)
            qs, ks32, vs32, _, _ = _qkv(xs, g_mix_pre[i], w, j,tabs[HD_B][1], rope_kind=HD_B, kw=kw, vw=vw, tm=db,
                                        q_scale=1.0)
            q4 = qs.reshape(db, KV_B, G_B, HD_B)
            qz = jnp.stack([jnp.pad(q4[:, kv], ((0, 0), (0, 0), (kv * HD_B, (KV_B - 1 - kv) * HD_B)))
                            for kv in range(KV_B)], axis=1)
            wb = state_win_k.shape[2]
            os_, nk, nv = _attn_b_sample(b_sink[j].reshape(KV_B, G_B, 1), qz, ks32.reshape(db, 1, kw),
                                         vs32.reshape(db, 1, vw), state_win_k[j].reshape(db, wb, kw),
                                         state_win_v[j].reshape(db, wb, vw), past_len=past_len)
            os_ = os_.reshape(db, H_B * HD_B)
            outs["b_kp"].append(k32.reshape(batch, seq, KV_B, HD_B)[:, seq - WINDOW:])
            outs["b_vp"].append(v32.reshape(batch, seq, KV_B, HD_B)[:, seq - WINDOW:])
            outs["b_ks"].append(nk.reshape(db, wb, KV_B, HD_B))
            outs["b_vs"].append(nv.reshape(db, wb, KV_B, HD_B))
        else:
            kw = vw = KV_C * HD_C
            q, k32, v32, k16, v16 = _qkv(xp, g_mix_pre[i], w, j,None, rope_kind=None, kw=kw, vw=vw, tm=tm,
                                         q_scale=scale_c)
            o = _attn_c_prompt(u, q, k16, v16, batch=batch, seq=seq, tq=tq)
            qs, ks32, vs32, _, _ = _qkv(xs, g_mix_pre[i], w, j,None, rope_kind=None, kw=kw, vw=vw, tm=db,
                                        q_scale=scale_c)
            q4 = qs.reshape(db, KV_C, G_C, HD_C)
            qz = jnp.stack([jnp.pad(q4[:, kv], ((0, 0), (kv * G_C, (KV_C - 1 - kv) * G_C), (0, 0)))
                            for kv in range(KV_C)], axis=1)
            os_ = _attn_c_sample(pt, u, qz, cache_c_k, cache_c_v, layer=j, pps=pps).reshape(db, H_C * HD_C)
            outs["c_kp"].append(k32.reshape(batch, seq, KV_C, HD_C))
            outs["c_vp"].append(v32.reshape(batch, seq, KV_C, HD_C))
            outs["c_ks"].append(ks32.reshape(db, 1, KV_C, HD_C))
            outs["c_vs"].append(vs32.reshape(db, 1, KV_C, HD_C))
        xp, xs = token_mixers(xp, xs, o, os_, w_out, j, i)
    state = [jnp.stack(outs[name]) for name in ("a_kp", "a_vp", "a_ks", "a_vs", "b_kp", "b_vp", "b_ks", "b_vs",
                                               "c_kp", "c_vp", "c_ks", "c_vs")]
    return (xp.reshape(batch, seq, D_MODEL), xs.reshape(db, 1, D_MODEL), *state)
```

```python
import functools
import math

import numpy as np
import jax
import jax.numpy as jnp
from jax import lax
from jax.experimental import pallas as pl
from jax.experimental.pallas import tpu as pltpu

D_MODEL = 2048
N_MIXERS = 3
HD_A, KV_A = 128, 2
H_A = D_MODEL // (2 * HD_A)
G_A = H_A // KV_A
VD_A = 2 * HD_A
HD_B, KV_B = 64, 4
H_B = D_MODEL // HD_B
G_B = H_B // KV_B
WINDOW = 128
HD_C, KV_C = 128, 4
H_C = D_MODEL // HD_C
G_C = H_C // KV_C
ROPE_THETA = 10000.0
EPS = 1e-6

LANES = 128
VMEM_LIMIT = 56 * 1024 * 1024
F32 = jnp.float32
BF16 = jnp.bfloat16
NEG_INF = float("-inf")
LOGITS_AHEAD = 2


def _params(sem):
    return pltpu.CompilerParams(dimension_semantics=sem, vmem_limit_bytes=VMEM_LIMIT)


def _rms(x, g):
    return x * lax.rsqrt(jnp.mean(x * x, axis=-1, keepdims=True) + EPS) * g


def _dot(a, b):
    return jnp.dot(a, b, preferred_element_type=F32)


def _dot_nt(a, b):
    return lax.dot_general(a, b, (((1,), (1,)), ((), ())), preferred_element_type=F32)


def _stick_logs(z2):
    w2 = jnp.maximum(z2, 0.0) + jnp.log2(1.0 + jnp.exp2(-jnp.abs(z2)))
    return z2 - w2, w2


def _qkv_kernel(*refs, rope_kind, kw, vw, q_scale, v_transposed, k_rows, v_rows):
    if rope_kind is None:
        x_ref, g_ref, w_ref, q_ref, k32_ref, v32_ref, k16_ref, v16_ref = refs
    else:
        x_ref, g_ref, w_ref, tab_ref, q_ref, k32_ref, v32_ref, k16_ref, v16_ref = refs
    xn = _rms(x_ref[...], g_ref[...]).astype(BF16)

    def rope(y):
        if rope_kind is None:
            return y
        if rope_kind == 128:
            return y * tab_ref[0] + pltpu.roll(y, 64, 1) * tab_ref[1]
        return (y * tab_ref[0] + pltpu.roll(y, 96, 1) * tab_ref[1]
                + pltpu.roll(y, 32, 1) * tab_ref[2])

    def project(c0, width, store):
        step = min(width, 4 * LANES)
        for s0 in range(0, width, step):
            y = _dot(xn, w_ref[:, c0 + s0:c0 + s0 + step])
            for t in range(step // LANES):
                store(s0 + t * LANES, y[:, t * LANES:(t + 1) * LANES])

    def store_q(c, y):
        y = rope(y)
        if q_scale != 1.0:
            y = y * q_scale
        q_ref[:, c:c + LANES] = y.astype(BF16)

    def store_f32(ref, c, y, rows):
        if rows is None:
            ref[:, c:c + LANES] = y
        else:
            tokens, groups = ref.shape[0], ref.shape[1]
            flat = ref.reshape(tokens * groups, LANES)
            flat[pl.ds(rows[c // LANES], tokens, stride=groups), :] = y

    def store_k(c, y):
        y = rope(y)
        store_f32(k32_ref, c, y, k_rows)
        k16_ref[:, c:c + LANES] = y.astype(BF16)

    def store_v(c, y):
        store_f32(v32_ref, c, y, v_rows)
        if v_transposed:
            v16_ref[c:c + LANES, :] = y.T.astype(BF16)
        else:
            v16_ref[:, c:c + LANES] = y.astype(BF16)

    project(0, D_MODEL, store_q)
    project(D_MODEL, kw, store_k)
    project(D_MODEL + kw, vw, store_v)


def _qkv(x, g, w, layer, tab, *, rope_kind, kw, vw, tm, q_scale, v_transposed=False,
         k_rows=None, v_rows=None):
    m = x.shape[0]

    def f32_out(width, rows):
        if rows is None:
            return (m, width), pl.BlockSpec((tm, width), lambda i: (i, 0))
        return (m, width // LANES, LANES), pl.BlockSpec((tm, width // LANES, LANES), lambda i: (i, 0, 0))

    (k32_shape, k32_spec), (v32_shape, v32_spec) = f32_out(kw, k_rows), f32_out(vw, v_rows)
    n = w.shape[2]
    grid = (m // tm,)
    in_specs = [pl.BlockSpec((tm, D_MODEL), lambda i: (i, 0)),
                pl.BlockSpec((1, D_MODEL), lambda i: (0, 0)),
                pl.BlockSpec((None, D_MODEL, n), lambda i: (layer, 0, 0))]
    args = [x, g.reshape(1, D_MODEL), w]
    if rope_kind is not None:
        nt = tab.shape[1] // tm
        in_specs.append(pl.BlockSpec((3, tm, LANES), lambda i: (0, i % nt, 0)))
        args.append(tab)
    if v_transposed:
        v16_shape, v16_spec = (vw, m), pl.BlockSpec((vw, tm), lambda i: (0, i))
    else:
        v16_shape, v16_spec = (m, vw), pl.BlockSpec((tm, vw), lambda i: (i, 0))
    out_shape = (jax.ShapeDtypeStruct((m, D_MODEL), BF16),
                 jax.ShapeDtypeStruct(k32_shape, F32), jax.ShapeDtypeStruct(v32_shape, F32),
                 jax.ShapeDtypeStruct((m, kw), BF16), jax.ShapeDtypeStruct(v16_shape, BF16))
    out_specs = (pl.BlockSpec((tm, D_MODEL), lambda i: (i, 0)), k32_spec, v32_spec,
                 pl.BlockSpec((tm, kw), lambda i: (i, 0)), v16_spec)
    return pl.pallas_call(
        functools.partial(_qkv_kernel, rope_kind=rope_kind, kw=kw, vw=vw, q_scale=q_scale,
                          v_transposed=v_transposed, k_rows=k_rows, v_rows=v_rows),
        grid=grid, in_specs=in_specs, out_specs=out_specs, out_shape=out_shape,
        compiler_params=_params(("parallel",)), name="qkv_proj")(*args)


def _rope_tables(pos, hd):
    half = hd // 2
    inv_freq = ROPE_THETA ** (-jnp.arange(half, dtype=F32) * 2.0 / hd)
    ang = pos.astype(F32)[:, None] * inv_freq[None, :]
    cos, sin = jnp.cos(ang), jnp.sin(ang)
    zero = jnp.zeros_like(sin)
    reps = LANES // hd
    cos_t = jnp.tile(jnp.concatenate([cos, cos], axis=-1), (1, reps))
    if hd == LANES:
        s1 = jnp.concatenate([-sin, sin], axis=-1)
        s2 = jnp.zeros_like(s1)
    else:
        s1 = jnp.tile(jnp.concatenate([-sin, zero], axis=-1), (1, reps))
        s2 = jnp.tile(jnp.concatenate([zero, sin], axis=-1), (1, reps))
    return jnp.stack([cos_t, s1, s2])


def _outproj_kernel(o_ref, w_ref, x_ref, g_ref, y_ref):
    y = _dot(o_ref[...].astype(BF16), w_ref[...])
    y_ref[...] = x_ref[...] + _rms(y, g_ref[...])


def _outproj(o, w, layer, x, g, *, tm):
    m = x.shape[0]
    return pl.pallas_call(
        _outproj_kernel, grid=(m // tm,),
        in_specs=[pl.BlockSpec((tm, D_MODEL), lambda i: (i, 0)),
                  pl.BlockSpec((None, D_MODEL, D_MODEL), lambda i: (layer, 0, 0)),
                  pl.BlockSpec((tm, D_MODEL), lambda i: (i, 0)),
                  pl.BlockSpec((1, D_MODEL), lambda i: (0, 0))],
        out_specs=pl.BlockSpec((tm, D_MODEL), lambda i: (i, 0)),
        out_shape=jax.ShapeDtypeStruct((m, D_MODEL), F32),
        compiler_params=_params(("parallel",)), name="out_proj")(o, w, x, g.reshape(1, D_MODEL))


def _mlp_kernel(x_ref, gpre_ref, wup_ref, wdn_ref, gpost_ref, y_ref, xn_sc, acc_sc):
    f = pl.program_id(1)

    @pl.when(f == 0)
    def _():
        xn_sc[...] = _rms(x_ref[...], gpre_ref[...]).astype(BF16)
        acc_sc[...] = jnp.zeros_like(acc_sc)

    h = jnp.maximum(_dot(xn_sc[...], wup_ref[...]), 0.0)
    acc_sc[...] += _dot((h * h).astype(BF16), wdn_ref[...])

    @pl.when(f == pl.num_programs(1) - 1)
    def _():
        y_ref[...] = x_ref[...] + _rms(acc_sc[...], gpost_ref[...])


def _mlp(x, gpre, wup, wdn, layer, gpost, *, tm, tf):
    m = x.shape[0]
    d_ff = wup.shape[2]
    return pl.pallas_call(
        _mlp_kernel, grid=(m // tm, d_ff // tf),
        in_specs=[pl.BlockSpec((tm, D_MODEL), lambda i, f: (i, 0)),
                  pl.BlockSpec((1, D_MODEL), lambda i, f: (0, 0)),
                  pl.BlockSpec((None, D_MODEL, tf), lambda i, f: (layer, 0, f)),
                  pl.BlockSpec((None, tf, D_MODEL), lambda i, f: (layer, f, 0)),
                  pl.BlockSpec((1, D_MODEL), lambda i, f: (0, 0))],
        out_specs=pl.BlockSpec((tm, D_MODEL), lambda i, f: (i, 0)),
        out_shape=jax.ShapeDtypeStruct((m, D_MODEL), F32),
        scratch_shapes=[pltpu.VMEM((tm, D_MODEL), BF16), pltpu.VMEM((tm, D_MODEL), F32)],
        compiler_params=_params(("parallel", "arbitrary")), name="mlp")(
            x, gpre.reshape(1, D_MODEL), wup, wdn, gpost.reshape(1, D_MODEL))


def _mlp_cast_kernel(x_ref, gpre_ref, wup32_ref, wdn32_ref, gpost_ref, y_ref, wup16_ref, wdn16_ref,
                     xn_sc, acc_sc):
    wup16_ref[...] = wup32_ref[...].astype(BF16)
    wdn16_ref[...] = wdn32_ref[...].astype(BF16)
    _mlp_kernel(x_ref, gpre_ref, wup16_ref, wdn16_ref, gpost_ref, y_ref, xn_sc, acc_sc)


def _mlp_cast(x, gpre, wup32, wdn32, layer, gpost, *, tf):
    m = x.shape[0]
    d_ff = wup32.shape[2]
    return pl.pallas_call(
        _mlp_cast_kernel, grid=(1, d_ff // tf),
        in_specs=[pl.BlockSpec((m, D_MODEL), lambda i, f: (0, 0)),
                  pl.BlockSpec((1, D_MODEL), lambda i, f: (0, 0)),
                  pl.BlockSpec((None, D_MODEL, tf), lambda i, f: (layer, 0, f)),
                  pl.BlockSpec((None, tf, D_MODEL), lambda i, f: (layer, f, 0)),
                  pl.BlockSpec((1, D_MODEL), lambda i, f: (0, 0))],
        out_specs=(pl.BlockSpec((m, D_MODEL), lambda i, f: (0, 0)),
                   pl.BlockSpec((None, D_MODEL, tf), lambda i, f: (0, 0, f)),
                   pl.BlockSpec((None, tf, D_MODEL), lambda i, f: (0, f, 0))),
        out_shape=(jax.ShapeDtypeStruct((m, D_MODEL), F32),
                   jax.ShapeDtypeStruct((1, D_MODEL, d_ff), BF16),
                   jax.ShapeDtypeStruct((1, d_ff, D_MODEL), BF16)),
        scratch_shapes=[pltpu.VMEM((m, D_MODEL), BF16), pltpu.VMEM((m, D_MODEL), F32)],
        compiler_params=_params(("arbitrary", "arbitrary")), name="mlp_cast")(
            x, gpre.reshape(1, D_MODEL), wup32, wdn32, gpost.reshape(1, D_MODEL))


def _diff_lambda(lam_ref, lam_init):
    d1 = jnp.sum(lam_ref[0:1, :] * lam_ref[1:2, :], axis=-1, keepdims=True)
    d2 = jnp.sum(lam_ref[2:3, :] * lam_ref[3:4, :], axis=-1, keepdims=True)
    return jnp.exp(d1) - jnp.exp(d2) + lam_init


def _causal_steps(nq, newest_first):
    pairs = [(i, j) for i in range(nq) for j in (range(i, -1, -1) if newest_first else range(i + 1))]
    qt, kt = zip(*pairs)
    return jnp.asarray(qt, jnp.int32), jnp.asarray(kt, jnp.int32)


def _attn_a_kernel(qt_ref, kt_ref, lam_ref, gsub_ref, q_ref, k_ref, vt_ref, o_ref, m_sc, l_sc, acc_sc,
                   *, lam_init):
    qi = qt_ref[pl.program_id(2)]
    ki = kt_ref[pl.program_id(2)]

    @pl.when(ki == 0)
    def _():
        m_sc[...] = jnp.full_like(m_sc, NEG_INF)
        l_sc[...] = jnp.zeros_like(l_sc)
        acc_sc[...] = jnp.zeros_like(acc_sc)

    def step(diag):
        vt = vt_ref[...]
        tq, tk = q_ref.shape[0], k_ref.shape[0]
        if diag:
            keep = (lax.broadcasted_iota(jnp.int32, (tk, tq), 0)
                    <= lax.broadcasted_iota(jnp.int32, (tk, tq), 1))
        def logits(idx):
            c = idx % 2
            return _dot_nt(k_ref[:, c * HD_A:(c + 1) * HD_A], q_ref[:, idx * HD_A:(idx + 1) * HD_A])

        queue = [logits(idx) for idx in range(LOGITS_AHEAD)]
        for idx in range(2 * G_A):
            s = queue.pop(0)
            if idx + LOGITS_AHEAD < 2 * G_A:
                queue.append(logits(idx + LOGITS_AHEAD))
            if diag:
                s = jnp.where(keep, s, NEG_INF)
            m_prev = m_sc[idx]
            m_new = jnp.maximum(m_prev, jnp.max(s, axis=0, keepdims=True))
            alpha = jnp.exp2(m_prev - m_new)
            p = jnp.exp2(s - m_new)
            l_sc[idx] = alpha * l_sc[idx] + jnp.sum(p, axis=0, keepdims=True)
            acc_sc[idx] = alpha * acc_sc[idx] + _dot(vt, p.astype(BF16))
            m_sc[idx] = m_new

    @pl.when(ki < qi)
    def _():
        step(False)

    @pl.when(ki == qi)
    def _():
        step(True)
        lam = _diff_lambda(lam_ref, lam_init)
        for g in range(G_A):
            ot = acc_sc[2 * g] / l_sc[2 * g] - lam * (acc_sc[2 * g + 1] / l_sc[2 * g + 1])
            o = _rms(ot.T, gsub_ref[...]) * (1.0 - lam_init)
            o_ref[:, g * VD_A:(g + 1) * VD_A] = o.astype(o_ref.dtype)


def _attn_a_prompt(q, k, vt, lamvec, gsub, *, batch, seq, lam_init, tq):
    nq = seq // tq
    gw = G_A * VD_A
    qt, kt = _causal_steps(nq, newest_first=False)
    gs = pltpu.PrefetchScalarGridSpec(
        num_scalar_prefetch=2, grid=(batch, KV_A, qt.shape[0]),
        in_specs=[pl.BlockSpec((4, HD_A), lambda b, h, t, qt, kt: (0, 0)),
                  pl.BlockSpec((1, VD_A), lambda b, h, t, qt, kt: (0, 0)),
                  pl.BlockSpec((tq, gw), lambda b, h, t, qt, kt: (b * nq + qt[t], h)),
                  pl.BlockSpec((tq, 2 * HD_A), lambda b, h, t, qt, kt: (b * nq + kt[t], h)),
                  pl.BlockSpec((VD_A, tq), lambda b, h, t, qt, kt: (h, b * nq + kt[t]))],
        out_specs=pl.BlockSpec((tq, gw), lambda b, h, t, qt, kt: (b * nq + qt[t], h)),
        scratch_shapes=[pltpu.VMEM((2 * G_A, 1, tq), F32), pltpu.VMEM((2 * G_A, 1, tq), F32),
                        pltpu.VMEM((2 * G_A, VD_A, tq), F32)])
    return pl.pallas_call(
        functools.partial(_attn_a_kernel, lam_init=lam_init), grid_spec=gs,
        out_shape=jax.ShapeDtypeStruct((batch * seq, H_A * VD_A), BF16),
        compiler_params=_params(("parallel", "parallel", "arbitrary")),
        name="diff_attn_prompt")(qt, kt, lamvec, gsub.reshape(1, VD_A), q, k, vt)


def _page_rows(ref, offset):
    tokens = ref.shape[0]
    stride = math.prod(ref.shape[1:-1])
    flat = ref.reshape(tokens * stride, ref.shape[-1])
    return flat[pl.ds(offset, tokens, stride=stride), :]


def _attn_a_sample_kernel(pt_ref, lam_ref, gsub_ref, qz_ref, kn_ref, vn_ref, *rest, pps, lam_init):
    k_refs, v_refs = rest[:pps], rest[pps:2 * pps]
    o_ref, m_sc, l_sc, acc_sc = rest[2 * pps:]
    st = pl.program_id(1)

    @pl.when(st == 0)
    def _():
        for kv in range(KV_A):
            kn = kn_ref[0, :, kv * 2 * HD_A:(kv + 1) * 2 * HD_A].astype(BF16).astype(F32)
            vn = vn_ref[0, :, kv * VD_A:(kv + 1) * VD_A].astype(BF16).astype(F32)
            s = jnp.sum(qz_ref[0, kv].astype(F32) * kn, axis=-1, keepdims=True)
            m_sc[kv] = s
            l_sc[kv] = jnp.ones_like(s)
            acc_sc[kv] = jnp.broadcast_to(vn, acc_sc.shape[1:])

    for kv in range(KV_A):
        k_kv = jnp.concatenate(
            [jnp.concatenate([_page_rows(r, 2 * kv + c).astype(BF16) for c in range(2)], axis=1)
             for r in k_refs], axis=0)
        v_kv = jnp.concatenate(
            [jnp.concatenate([_page_rows(r, h * KV_A + kv).astype(BF16) for h in range(VD_A // LANES)], axis=1)
             for r in v_refs], axis=0)
        s = _dot_nt(qz_ref[0, kv], k_kv)
        m_prev = m_sc[kv]
        m_new = jnp.maximum(m_prev, jnp.max(s, axis=-1, keepdims=True))
        alpha = jnp.exp2(m_prev - m_new)
        p = jnp.exp2(s - m_new)
        l_sc[kv] = alpha * l_sc[kv] + jnp.sum(p, axis=-1, keepdims=True)
        acc_sc[kv] = alpha * acc_sc[kv] + _dot(p.astype(BF16), v_kv)
        m_sc[kv] = m_new

    @pl.when(st == pl.num_programs(1) - 1)
    def _():
        lam = _diff_lambda(lam_ref, lam_init)
        for kv in range(KV_A):
            on = acc_sc[kv] / l_sc[kv]
            o = on[0:G_A] - lam * on[G_A:2 * G_A]
            o_ref[0, kv] = _rms(o, gsub_ref[...]) * (1.0 - lam_init)


def _attn_a_sample(pt, lamvec, gsub, qz, kn, vn, cache_k, cache_v, *, layer, lam_init, pps):
    db = qz.shape[0]
    n_pages = pt.shape[0] // db
    page = cache_k.shape[2]

    def page_map(i, tail):
        return lambda b, s, pt_ref: (layer, pt_ref[b * n_pages + s * pps + i]) + tail

    in_specs = [pl.BlockSpec((4, HD_A), lambda b, s, pt_ref: (0, 0)),
                pl.BlockSpec((1, VD_A), lambda b, s, pt_ref: (0, 0)),
                pl.BlockSpec((1, KV_A, 2 * G_A, 2 * HD_A), lambda b, s, pt_ref: (b, 0, 0, 0)),
                pl.BlockSpec((1, 1, KV_A * 2 * HD_A), lambda b, s, pt_ref: (b, 0, 0)),
                pl.BlockSpec((1, 1, KV_A * VD_A), lambda b, s, pt_ref: (b, 0, 0))]
    in_specs += [pl.BlockSpec((None, None, page, KV_A, 2, HD_A), page_map(i, (0, 0, 0, 0)))
                 for i in range(pps)]
    in_specs += [pl.BlockSpec((None, None, page, VD_A // LANES, KV_A, LANES), page_map(i, (0, 0, 0, 0)))
                 for i in range(pps)]
    gs = pltpu.PrefetchScalarGridSpec(
        num_scalar_prefetch=1, grid=(db, n_pages // pps), in_specs=in_specs,
        out_specs=pl.BlockSpec((1, KV_A, G_A, VD_A), lambda b, s, pt_ref: (b, 0, 0, 0)),
        scratch_shapes=[pltpu.VMEM((KV_A, 2 * G_A, 1), F32), pltpu.VMEM((KV_A, 2 * G_A, 1), F32),
                        pltpu.VMEM((KV_A, 2 * G_A, VD_A), F32)])
    return pl.pallas_call(
        functools.partial(_attn_a_sample_kernel, pps=pps, lam_init=lam_init),
        grid_spec=gs, out_shape=jax.ShapeDtypeStruct((db, KV_A, G_A, VD_A), F32),
        compiler_params=_params(("parallel", "arbitrary")), name="diff_attn_sample")(
            pt, lamvec, gsub.reshape(1, VD_A), qz, kn, vn,
            *([cache_k] * pps), *([cache_v] * pps))


def _attn_b_kernel(sink_ref, q_ref, kc_ref, kp_ref, vc_ref, vp_ref, o_ref):
    i = pl.program_id(1)
    w = q_ref.shape[0]
    scale = HD_B ** -0.5
    kk = jnp.concatenate([kp_ref[...], kc_ref[...]], axis=0)
    vv = jnp.concatenate([vp_ref[...], vc_ref[...]], axis=0)
    row = lax.broadcasted_iota(jnp.int32, (w, 2 * w), 0)
    col = lax.broadcasted_iota(jnp.int32, (w, 2 * w), 1)
    keep = (col >= row) & (col <= row + WINDOW) & ((i > 0) | (col >= w))
    for kv in range(KV_B):
        k = kk[:, kv * HD_B:(kv + 1) * HD_B]
        v = vv[:, kv * HD_B:(kv + 1) * HD_B]
        outs = []
        for g in range(G_B):
            h = kv * G_B + g
            q = q_ref[:, h * HD_B:(h + 1) * HD_B]
            s = jnp.where(keep, _dot_nt(q, k) * scale, NEG_INF)
            sink = sink_ref[h]
            mx = jnp.maximum(jnp.max(s, axis=-1, keepdims=True), sink)
            e = jnp.exp(s - mx)
            denom = jnp.sum(e, axis=-1, keepdims=True) + jnp.exp(sink - mx)
            outs.append(_dot(e.astype(BF16), v) / denom)
        for t in range(G_B // 2):
            h0 = kv * G_B + 2 * t
            o_ref[:, h0 * HD_B:(h0 + 2) * HD_B] = jnp.concatenate(
                [outs[2 * t], outs[2 * t + 1]], axis=1).astype(o_ref.dtype)


def _attn_b_prompt(sink, q, k, v, *, batch, seq):
    nb = seq // WINDOW
    kvw = KV_B * HD_B
    cur = lambda b, i: (b * nb + i, 0)
    prev = lambda b, i: (b * nb + jnp.maximum(i - 1, 0), 0)
    return pl.pallas_call(
        _attn_b_kernel, grid=(batch, nb),
        in_specs=[pl.BlockSpec(memory_space=pltpu.SMEM),
                  pl.BlockSpec((WINDOW, D_MODEL), cur),
                  pl.BlockSpec((WINDOW, kvw), cur), pl.BlockSpec((WINDOW, kvw), prev),
                  pl.BlockSpec((WINDOW, kvw), cur), pl.BlockSpec((WINDOW, kvw), prev)],
        out_specs=pl.BlockSpec((WINDOW, D_MODEL), cur),
        out_shape=jax.ShapeDtypeStruct((batch * seq, D_MODEL), BF16),
        compiler_params=_params(("parallel", "parallel")), name="swa_prompt")(sink, q, k, k, v, v)


def _attn_b_sample_kernel(sink_ref, qz_ref, kn_ref, vn_ref, bk_ref, bv_ref, o_ref, nk_ref, nv_ref,
                          *, past_len):
    wb = bk_ref.shape[1]
    scale = HD_B ** -0.5
    bk = bk_ref[0]
    bv = bv_ref[0]
    kn = kn_ref[0]
    vn = vn_ref[0]
    nk_ref[0, 0:wb - 1, :] = bk[1:wb, :]
    nk_ref[0, wb - 1:wb, :] = kn
    nv_ref[0, 0:wb - 1, :] = bv[1:wb, :]
    nv_ref[0, wb - 1:wb, :] = vn
    dist = wb - lax.broadcasted_iota(jnp.int32, (G_B, wb), 1)
    keep = (dist >= 0) & (dist <= WINDOW) & (past_len - dist >= 0)
    bk16 = bk.astype(BF16)
    bv16 = bv.astype(BF16)
    kn16 = kn.astype(BF16).astype(F32)
    vn16 = vn.astype(BF16).astype(F32)
    for kv in range(KV_B):
        qz = qz_ref[0, kv]
        s = jnp.where(keep, _dot_nt(qz, bk16) * scale, NEG_INF)
        s_new = jnp.sum(qz.astype(F32) * kn16, axis=-1, keepdims=True) * scale
        sink = sink_ref[kv]
        mx = jnp.maximum(jnp.maximum(jnp.max(s, axis=-1, keepdims=True), s_new), sink)
        e = jnp.exp(s - mx)
        e_new = jnp.exp(s_new - mx)
        denom = jnp.sum(e, axis=-1, keepdims=True) + e_new + jnp.exp(sink - mx)
        o = _dot(e.astype(BF16), bv16) + e_new.astype(BF16).astype(F32) * vn16
        o_ref[0, kv] = (o / denom)[:, kv * HD_B:(kv + 1) * HD_B]


def _attn_b_sample(sink, qz, kn, vn, buf_k, buf_v, *, past_len):
    db, wb, kvw = buf_k.shape
    row = lambda b: (b, 0, 0)
    return pl.pallas_call(
        functools.partial(_attn_b_sample_kernel, past_len=past_len), grid=(db,),
        in_specs=[pl.BlockSpec((KV_B, G_B, 1), lambda b: (0, 0, 0)),
                  pl.BlockSpec((1, KV_B, G_B, kvw), lambda b: (b, 0, 0, 0)),
                  pl.BlockSpec((1, 1, kvw), row), pl.BlockSpec((1, 1, kvw), row),
                  pl.BlockSpec((1, wb, kvw), row), pl.BlockSpec((1, wb, kvw), row)],
        out_specs=(pl.BlockSpec((1, KV_B, G_B, HD_B), lambda b: (b, 0, 0, 0)),
                   pl.BlockSpec((1, wb, kvw), row), pl.BlockSpec((1, wb, kvw), row)),
        out_shape=(jax.ShapeDtypeStruct((db, KV_B, G_B, HD_B), F32),
                   jax.ShapeDtypeStruct((db, wb, kvw), F32), jax.ShapeDtypeStruct((db, wb, kvw), F32)),
        compiler_params=_params(("parallel",)), name="swa_sample")(sink, qz, kn, vn, buf_k, buf_v)


def _suffix_matrix():
    j = np.arange(LANES)[:, None]
    s = np.arange(LANES)[None, :]
    u = np.concatenate([(j > s).astype(np.float32), np.ones((LANES, LANES), np.float32)], axis=1)
    return jnp.asarray(np.concatenate([u, u], axis=0), dtype=BF16)


def _suffix_sums(w, u):
    hi = w.astype(BF16)
    lo = (w - hi.astype(F32)).astype(BF16)
    return _dot(jnp.concatenate([hi, lo], axis=1), u)


def _attn_c_kernel(qt_ref, kt_ref, u_ref, q_ref, k_ref, v_ref, o_ref, carry_sc, acc_sc):
    qi = qt_ref[pl.program_id(2)]
    kb = kt_ref[pl.program_id(2)]

    @pl.when(kb == qi)
    def _():
        carry_sc[...] = jnp.zeros_like(carry_sc)
        acc_sc[...] = jnp.zeros_like(acc_sc)

    def step(diag):
        k = k_ref[...]
        v = v_ref[...]
        u = u_ref[...]
        tq, tk = q_ref.shape[0], k.shape[0]
        if diag:
            keep = (lax.broadcasted_iota(jnp.int32, (tq, tk), 1)
                    < lax.broadcasted_iota(jnp.int32, (tq, tk), 0))
        def logits(g):
            return _dot_nt(q_ref[:, g * HD_C:(g + 1) * HD_C], k)

        z_next = logits(0)
        for g in range(G_C):
            z = z_next
            if g + 1 < G_C:
                z_next = logits(g + 1)
            lb, w = _stick_logs(z)
            if diag:
                w = jnp.where(keep, w, 0.0)
            carry = carry_sc[g]
            a_blocks = [None] * (tk // LANES)
            for sb in reversed(range(tk // LANES)):
                cols = slice(sb * LANES, (sb + 1) * LANES)
                sums = _suffix_sums(w[:, cols], u)
                a = jnp.exp2(lb[:, cols] - (carry + sums[:, :LANES]))
                if diag:
                    a = jnp.where(keep[:, cols], a, 0.0)
                a_blocks[sb] = a.astype(BF16)
                carry = carry + sums[:, LANES:]
            carry_sc[g] = carry
            acc_sc[g] += _dot(jnp.concatenate(a_blocks, axis=1), v)

    @pl.when(kb == qi)
    def _():
        step(True)

    @pl.when(kb < qi)
    def _():
        step(False)

    @pl.when(kb == 0)
    def _():
        for g in range(G_C):
            o_ref[:, g * HD_C:(g + 1) * HD_C] = acc_sc[g].astype(o_ref.dtype)


def _attn_c_prompt(u, q, k, v, *, batch, seq, tq):
    nq = seq // tq
    gw = G_C * HD_C
    qt, kt = _causal_steps(nq, newest_first=True)
    kmap = lambda b, h, t, qt, kt: (b * nq + kt[t], h)
    gs = pltpu.PrefetchScalarGridSpec(
        num_scalar_prefetch=2, grid=(batch, KV_C, qt.shape[0]),
        in_specs=[pl.BlockSpec((2 * LANES, 2 * LANES), lambda b, h, t, qt, kt: (0, 0)),
                  pl.BlockSpec((tq, gw), lambda b, h, t, qt, kt: (b * nq + qt[t], h)),
                  pl.BlockSpec((tq, HD_C), kmap), pl.BlockSpec((tq, HD_C), kmap)],
        out_specs=pl.BlockSpec((tq, gw), lambda b, h, t, qt, kt: (b * nq + qt[t], h)),
        scratch_shapes=[pltpu.VMEM((G_C, tq, LANES), F32), pltpu.VMEM((G_C, tq, HD_C), F32)])
    return pl.pallas_call(
        _attn_c_kernel, grid_spec=gs,
        out_shape=jax.ShapeDtypeStruct((batch * seq, H_C * HD_C), BF16),
        compiler_params=_params(("parallel", "parallel", "arbitrary")),
        name="sb_attn_prompt")(qt, kt, u, q, k, v)


def _attn_c_sample_kernel(pt_ref, u_ref, qz_ref, *rest, pps):
    k_refs, v_refs = rest[:pps], rest[pps:2 * pps]
    o_ref, carry_sc, acc_sc = rest[2 * pps:]
    st = pl.program_id(1)

    @pl.when(st == 0)
    def _():
        carry_sc[...] = jnp.zeros_like(carry_sc)
        acc_sc[...] = jnp.zeros_like(acc_sc)

    page = k_refs[0].shape[0]
    z = None
    for kv in range(KV_C):
        k_kv = jnp.concatenate([_page_rows(r, kv).astype(BF16) for r in k_refs], axis=0)
        zk = _dot_nt(qz_ref[0, kv], k_kv)
        z = zk if z is None else z + zk
    lb, w = _stick_logs(z)
    w_rows = jnp.concatenate([w[:, i * page:(i + 1) * page] for i in range(pps)], axis=0)
    sums = _suffix_sums(w_rows, u_ref[...])
    carry = carry_sc[...]
    a_blocks = [None] * pps
    for i in reversed(range(pps)):
        rows = slice(i * H_C, (i + 1) * H_C)
        a_blocks[i] = jnp.exp2(lb[:, i * page:(i + 1) * page] - (carry + sums[rows, :LANES]))
        carry = carry + sums[rows, LANES:]
    carry_sc[...] = carry
    a = jnp.concatenate(a_blocks, axis=1)
    row_kv = lax.broadcasted_iota(jnp.int32, a.shape, 0) // G_C
    acc = acc_sc[...]
    for kv in range(KV_C):
        v_kv = jnp.concatenate([_page_rows(r, kv).astype(BF16) for r in v_refs], axis=0)
        acc = acc + _dot(jnp.where(row_kv == kv, a, 0.0).astype(BF16), v_kv)
    acc_sc[...] = acc

    @pl.when(st == pl.num_programs(1) - 1)
    def _():
        o_ref[0] = acc


def _attn_c_sample(pt, u, qz, cache_k, cache_v, *, layer, pps):
    db = qz.shape[0]
    n_pages = pt.shape[0] // db
    n_steps = n_pages // pps
    page = cache_k.shape[2]

    def page_map(i):
        return lambda b, s, pt_ref: (layer, pt_ref[b * n_pages + (n_steps - 1 - s) * pps + i], 0, 0, 0)

    in_specs = [pl.BlockSpec((2 * LANES, 2 * LANES), lambda b, s, pt_ref: (0, 0)),
                pl.BlockSpec((1, KV_C, H_C, HD_C), lambda b, s, pt_ref: (b, 0, 0, 0))]
    in_specs += [pl.BlockSpec((None, None, page, KV_C, HD_C), page_map(i)) for i in range(pps)] * 2
    gs = pltpu.PrefetchScalarGridSpec(
        num_scalar_prefetch=1, grid=(db, n_steps), in_specs=in_specs,
        out_specs=pl.BlockSpec((1, H_C, HD_C), lambda b, s, pt_ref: (b, 0, 0)),
        scratch_shapes=[pltpu.VMEM((H_C, LANES), F32), pltpu.VMEM((H_C, HD_C), F32)])
    return pl.pallas_call(
        functools.partial(_attn_c_sample_kernel, pps=pps),
        grid_spec=gs, out_shape=jax.ShapeDtypeStruct((db, H_C, HD_C), F32),
        compiler_params=_params(("parallel", "arbitrary")), name="sb_attn_sample")(
            pt, u, qz, *([cache_k] * pps), *([cache_v] * pps))


def _pick(n, pref):
    return pref if n % pref == 0 else n


def kernel(x_prompt, x_sample, cache_a_k, cache_a_v, cache_c_k, cache_c_v, state_win_k, state_win_v, page_table, g_mix_pre, g_mix_post, g_ffn_pre, g_ffn_post, w_a_qkv, lam_q1, lam_k1, lam_q2, lam_k2, g_a_sub, w_a_out, w_b_qkv, b_sink, w_b_out, w_c_qkv, w_c_out, w_up, w_down):
    batch, seq, _ = x_prompt.shape
    db, dseq, _ = x_sample.shape
    assert dseq == 1
    depth = g_mix_pre.shape[0]
    n_pages = page_table.shape[1]
    page = cache_a_k.shape[2]
    past_len = n_pages * page
    m_p = batch * seq
    tm = _pick(m_p, 256)
    tq = _pick(seq, 512)
    tm_mlp = _pick(m_p, 512)
    pps = _pick(n_pages, 16)
    scale_a = HD_A ** -0.5 * math.log2(math.e)
    scale_c = HD_C ** -0.5 * math.log2(math.e)

    xp = x_prompt.reshape(m_p, D_MODEL)
    xs = x_sample.reshape(db, D_MODEL)
    pt = page_table.reshape(-1)
    pos_p = jnp.arange(seq, dtype=jnp.int32)
    pos_s = jnp.full((db,), past_len, dtype=jnp.int32)
    tabs = {hd: (_rope_tables(pos_p, hd), _rope_tables(pos_s, hd)) for hd in (HD_A, HD_B)}
    u = _suffix_matrix()
    cache_a_v6 = cache_a_v.reshape(*cache_a_v.shape[:3], KV_A, VD_A // LANES, LANES).transpose(0, 1, 2, 4, 3, 5)
    v_rows_a = tuple(half * KV_A + head for head in range(KV_A) for half in range(VD_A // LANES))

    w_qkv16 = (w_a_qkv.astype(BF16), w_b_qkv.astype(BF16), w_c_qkv.astype(BF16))
    w_out16 = (w_a_out.astype(BF16), w_b_out.astype(BF16), w_c_out.astype(BF16))

    def token_mixers(xp, xs, o, os_, w_out, j, i):
        xs = _outproj(os_, w_out, j, xs, g_mix_post[i], tm=db)
        xs, w_up16, w_down16 = _mlp_cast(xs, g_ffn_pre[i], w_up, w_down, i, g_ffn_post[i], tf=512)
        xp = _outproj(o, w_out, j, xp, g_mix_post[i], tm=tm)
        xp = _mlp(xp, g_ffn_pre[i], w_up16, w_down16, 0, g_ffn_post[i], tm=tm_mlp, tf=512)
        return xp, xs

    outs = {name: [] for name in ("a_kp", "a_vp", "a_ks", "a_vs", "b_kp", "b_vp", "b_ks", "b_vs",
                                  "c_kp", "c_vp", "c_ks", "c_vs")}
    for i in range(depth):
        kind, j = i % N_MIXERS, i // N_MIXERS
        w, w_out = w_qkv16[kind], w_out16[kind]
        if kind == 0:
            lam_init = 0.8 - 0.6 * math.exp(-0.3 * i)
            lamvec = jnp.stack([lam_q1[j], lam_k1[j], lam_q2[j], lam_k2[j]])
            kw, vw = KV_A * 2 * HD_A, KV_A * VD_A
            q, k32, v32, k16, v16t = _qkv(xp, g_mix_pre[i], w, j,tabs[HD_A][0], rope_kind=HD_A, kw=kw, vw=vw, tm=tm,
                                          q_scale=scale_a, v_transposed=True,
                                          k_rows=tuple(range(kw // LANES)), v_rows=v_rows_a)
            o = _attn_a_prompt(q, k16, v16t, lamvec, g_a_sub[j], batch=batch, seq=seq, lam_init=lam_init, tq=tq)
            qs, ks32, vs32, _, _ = _qkv(xs, g_mix_pre[i], w, j,tabs[HD_A][1], rope_kind=HD_A, kw=kw, vw=vw, tm=db,
                                        q_scale=scale_a)
            q5 = qs.reshape(db, KV_A, G_A, 2, HD_A)
            zero = jnp.zeros((db, KV_A, G_A, HD_A), BF16)
            qz = jnp.concatenate([jnp.concatenate([q5[:, :, :, 0], zero], axis=-1),
                                  jnp.concatenate([zero, q5[:, :, :, 1]], axis=-1)], axis=2)
            os_ = _attn_a_sample(pt, lamvec, g_a_sub[j], qz, ks32.reshape(db, 1, kw), vs32.reshape(db, 1, vw),
                                 cache_a_k, cache_a_v6, layer=j, lam_init=lam_init, pps=pps)
            os_ = os_.reshape(db, H_A * VD_A)
            outs["a_kp"].append(k32.reshape(batch, seq, KV_A, 2, HD_A))
            outs["a_vp"].append(v32.reshape(batch, seq, VD_A // LANES, KV_A, LANES)
                                .transpose(0, 1, 3, 2, 4).reshape(batch, seq, KV_A, VD_A))
            outs["a_ks"].append(ks32.reshape(db, 1, KV_A, 2, HD_A))
            outs["a_vs"].append(vs32.reshape(db, 1, KV_A, VD_A))
        elif kind == 1:
            kw = vw = KV_B * HD_B
            q, k32, v32, k16, v16 = _qkv(xp, g_mix_pre[i], w, j,tabs[HD_B][0], rope_kind=HD_B, kw=kw, vw=vw, tm=tm,
                                         q_scale=1.0)
            o = _attn_b_prompt(b_sink[j], q, k16, v16, batch=batch, seq=seq)
            qs, ks32, vs32, _, _ = _qkv(xs, g_mix_pre[i], w, j,tabs[HD_B][1], rope_kind=HD_B, kw=kw, vw=vw, tm=db,
                                        q_scale=1.0)
            q4 = qs.reshape(db, KV_B, G_B, HD_B)
            qz = jnp.stack([jnp.pad(q4[:, kv], ((0, 0), (0, 0), (kv * HD_B, (KV_B - 1 - kv) * HD_B)))
                            for kv in range(KV_B)], axis=1)
            wb = state_win_k.shape[2]
            os_, nk, nv = _attn_b_sample(b_sink[j].reshape(KV_B, G_B, 1), qz, ks32.reshape(db, 1, kw),
                                         vs32.reshape(db, 1, vw), state_win_k[j].reshape(db, wb, kw),
                                         state_win_v[j].reshape(db, wb, vw), past_len=past_len)
            os_ = os_.reshape(db, H_B * HD_B)
            outs["b_kp"].append(k32.reshape(batch, seq, KV_B, HD_B)[:, seq - WINDOW:])
            outs["b_vp"].append(v32.reshape(batch, seq, KV_B, HD_B)[:, seq - WINDOW:])
            outs["b_ks"].append(nk.reshape(db, wb, KV_B, HD_B))
            outs["b_vs"].append(nv.reshape(db, wb, KV_B, HD_B))
        else:
            kw = vw = KV_C * HD_C
            q, k32, v32, k16, v16 = _qkv(xp, g_mix_pre[i], w, j, None, rope_kind=None, kw=kw, vw=vw, tm=tm,
                                         q_scale=scale_c, k_rows=tuple(range(kw // LANES)),
                                         v_rows=tuple(range(vw // LANES)))
            o = _attn_c_prompt(u, q, k16, v16, batch=batch, seq=seq, tq=tq)
            qs, ks32, vs32, _, _ = _qkv(xs, g_mix_pre[i], w, j,None, rope_kind=None, kw=kw, vw=vw, tm=db,
                                        q_scale=scale_c)
            q4 = qs.reshape(db, KV_C, G_C, HD_C)
            qz = jnp.stack([jnp.pad(q4[:, kv], ((0, 0), (kv * G_C, (KV_C - 1 - kv) * G_C), (0, 0)))
                            for kv in range(KV_C)], axis=1)
            os_ = _attn_c_sample(pt, u, qz, cache_c_k, cache_c_v, layer=j, pps=pps).reshape(db, H_C * HD_C)
            outs["c_kp"].append(k32.reshape(batch, seq, KV_C, HD_C))
            outs["c_vp"].append(v32.reshape(batch, seq, KV_C, HD_C))
            outs["c_ks"].append(ks32.reshape(db, 1, KV_C, HD_C))
            outs["c_vs"].append(vs32.reshape(db, 1, KV_C, HD_C))
        xp, xs = token_mixers(xp, xs, o, os_, w_out, j, i)
    state = [jnp.stack(outs[name]) for name in ("a_kp", "a_vp", "a_ks", "a_vs", "b_kp", "b_vp", "b_ks", "b_vs",
                                               "c_kp", "c_vp", "c_ks", "c_vs")]
    return (xp.reshape(batch, seq, D_MODEL), xs.reshape(db, 1, D_MODEL), *state)
```

```python
import functools
import math

import numpy as np
import jax
import jax.numpy as jnp
from jax import lax
from jax.experimental import pallas as pl
from jax.experimental.pallas import tpu as pltpu

D_MODEL = 2048
N_MIXERS = 3
HD_A, KV_A = 128, 2
H_A = D_MODEL // (2 * HD_A)
G_A = H_A // KV_A
VD_A = 2 * HD_A
HD_B, KV_B = 64, 4
H_B = D_MODEL // HD_B
G_B = H_B // KV_B
WINDOW = 128
HD_C, KV_C = 128, 4
H_C = D_MODEL // HD_C
G_C = H_C // KV_C
ROPE_THETA = 10000.0
EPS = 1e-6

LANES = 128
VMEM_LIMIT = 56 * 1024 * 1024
F32 = jnp.float32
BF16 = jnp.bfloat16
NEG_INF = float("-inf")
LOGITS_AHEAD = 2


def _params(sem):
    return pltpu.CompilerParams(dimension_semantics=sem, vmem_limit_bytes=VMEM_LIMIT)


def _rms(x, g):
    return x * lax.rsqrt(jnp.mean(x * x, axis=-1, keepdims=True) + EPS) * g


def _dot(a, b):
    return jnp.dot(a, b, preferred_element_type=F32)


def _dot_nt(a, b):
    return lax.dot_general(a, b, (((1,), (1,)), ((), ())), preferred_element_type=F32)


def _stick_logs(z2):
    w2 = jnp.maximum(z2, 0.0) + jnp.log2(1.0 + jnp.exp2(-jnp.abs(z2)))
    return z2 - w2, w2


def _qkv_kernel(*refs, rope_kind, kw, vw, q_scale, v_transposed, k_rows, v_rows):
    if rope_kind is None:
        x_ref, g_ref, w_ref, q_ref, k32_ref, v32_ref, k16_ref, v16_ref = refs
    else:
        x_ref, g_ref, w_ref, tab_ref, q_ref, k32_ref, v32_ref, k16_ref, v16_ref = refs
    xn = _rms(x_ref[...], g_ref[...]).astype(BF16)

    def rope(y):
        if rope_kind is None:
            return y
        if rope_kind == 128:
            return y * tab_ref[0] + pltpu.roll(y, 64, 1) * tab_ref[1]
        return (y * tab_ref[0] + pltpu.roll(y, 96, 1) * tab_ref[1]
                + pltpu.roll(y, 32, 1) * tab_ref[2])

    def project(c0, width, store):
        step = min(width, 4 * LANES)
        for s0 in range(0, width, step):
            y = _dot(xn, w_ref[:, c0 + s0:c0 + s0 + step])
            for t in range(step // LANES):
                store(s0 + t * LANES, y[:, t * LANES:(t + 1) * LANES])

    def store_q(c, y):
        y = rope(y)
        if q_scale != 1.0:
            y = y * q_scale
        q_ref[:, c:c + LANES] = y.astype(BF16)

    def store_f32(ref, c, y, rows):
        if rows is None:
            ref[:, c:c + LANES] = y
        else:
            tokens, groups = ref.shape[0], ref.shape[1]
            flat = ref.reshape(tokens * groups, LANES)
            flat[pl.ds(rows[c // LANES], tokens, stride=groups), :] = y

    def store_k(c, y):
        y = rope(y)
        store_f32(k32_ref, c, y, k_rows)
        k16_ref[:, c:c + LANES] = y.astype(BF16)

    def store_v(c, y):
        store_f32(v32_ref, c, y, v_rows)
        if v_transposed:
            v16_ref[c:c + LANES, :] = y.T.astype(BF16)
        else:
            v16_ref[:, c:c + LANES] = y.astype(BF16)

    project(0, D_MODEL, store_q)
    project(D_MODEL, kw, store_k)
    project(D_MODEL + kw, vw, store_v)


def _qkv(x, g, w, layer, tab, *, rope_kind, kw, vw, tm, q_scale, v_transposed=False,
         k_rows=None, v_rows=None):
    m = x.shape[0]

    def f32_out(width, rows):
        if rows is None:
            return (m, width), pl.BlockSpec((tm, width), lambda i: (i, 0))
        return (m, width // LANES, LANES), pl.BlockSpec((tm, width // LANES, LANES), lambda i: (i, 0, 0))

    (k32_shape, k32_spec), (v32_shape, v32_spec) = f32_out(kw, k_rows), f32_out(vw, v_rows)
    n = w.shape[2]
    grid = (m // tm,)
    in_specs = [pl.BlockSpec((tm, D_MODEL), lambda i: (i, 0)),
                pl.BlockSpec((1, D_MODEL), lambda i: (0, 0)),
                pl.BlockSpec((None, D_MODEL, n), lambda i: (layer, 0, 0))]
    args = [x, g.reshape(1, D_MODEL), w]
    if rope_kind is not None:
        nt = tab.shape[1] // tm
        in_specs.append(pl.BlockSpec((3, tm, LANES), lambda i: (0, i % nt, 0)))
        args.append(tab)
    if v_transposed:
        v16_shape, v16_spec = (vw, m), pl.BlockSpec((vw, tm), lambda i: (0, i))
    else:
        v16_shape, v16_spec = (m, vw), pl.BlockSpec((tm, vw), lambda i: (i, 0))
    out_shape = (jax.ShapeDtypeStruct((m, D_MODEL), BF16),
                 jax.ShapeDtypeStruct(k32_shape, F32), jax.ShapeDtypeStruct(v32_shape, F32),
                 jax.ShapeDtypeStruct((m, kw), BF16), jax.ShapeDtypeStruct(v16_shape, BF16))
    out_specs = (pl.BlockSpec((tm, D_MODEL), lambda i: (i, 0)), k32_spec, v32_spec,
                 pl.BlockSpec((tm, kw), lambda i: (i, 0)), v16_spec)
    return pl.pallas_call(
        functools.partial(_qkv_kernel, rope_kind=rope_kind, kw=kw, vw=vw, q_scale=q_scale,
                          v_transposed=v_transposed, k_rows=k_rows, v_rows=v_rows),
        grid=grid, in_specs=in_specs, out_specs=out_specs, out_shape=out_shape,
        compiler_params=_params(("parallel",)), name="qkv_proj")(*args)


def _rope_tables(pos, hd):
    half = hd // 2
    inv_freq = ROPE_THETA ** (-jnp.arange(half, dtype=F32) * 2.0 / hd)
    ang = pos.astype(F32)[:, None] * inv_freq[None, :]
    cos, sin = jnp.cos(ang), jnp.sin(ang)
    zero = jnp.zeros_like(sin)
    reps = LANES // hd
    cos_t = jnp.tile(jnp.concatenate([cos, cos], axis=-1), (1, reps))
    if hd == LANES:
        s1 = jnp.concatenate([-sin, sin], axis=-1)
        s2 = jnp.zeros_like(s1)
    else:
        s1 = jnp.tile(jnp.concatenate([-sin, zero], axis=-1), (1, reps))
        s2 = jnp.tile(jnp.concatenate([zero, sin], axis=-1), (1, reps))
    return jnp.stack([cos_t, s1, s2])


def _outproj_kernel(o_ref, w_ref, x_ref, g_ref, y_ref):
    y = _dot(o_ref[...].astype(BF16), w_ref[...])
    y_ref[...] = x_ref[...] + _rms(y, g_ref[...])


def _outproj(o, w, layer, x, g, *, tm):
    m = x.shape[0]
    return pl.pallas_call(
        _outproj_kernel, grid=(m // tm,),
        in_specs=[pl.BlockSpec((tm, D_MODEL), lambda i: (i, 0)),
                  pl.BlockSpec((None, D_MODEL, D_MODEL), lambda i: (layer, 0, 0)),
                  pl.BlockSpec((tm, D_MODEL), lambda i: (i, 0)),
                  pl.BlockSpec((1, D_MODEL), lambda i: (0, 0))],
        out_specs=pl.BlockSpec((tm, D_MODEL), lambda i: (i, 0)),
        out_shape=jax.ShapeDtypeStruct((m, D_MODEL), F32),
        compiler_params=_params(("parallel",)), name="out_proj")(o, w, x, g.reshape(1, D_MODEL))


def _mlp_kernel(x_ref, gpre_ref, wup_ref, wdn_ref, gpost_ref, y_ref, xn_sc, acc_sc):
    f = pl.program_id(1)

    @pl.when(f == 0)
    def _():
        xn_sc[...] = _rms(x_ref[...], gpre_ref[...]).astype(BF16)
        acc_sc[...] = jnp.zeros_like(acc_sc)

    h = jnp.maximum(_dot(xn_sc[...], wup_ref[...]), 0.0)
    acc_sc[...] += _dot((h * h).astype(BF16), wdn_ref[...])

    @pl.when(f == pl.num_programs(1) - 1)
    def _():
        y_ref[...] = x_ref[...] + _rms(acc_sc[...], gpost_ref[...])


def _mlp(x, gpre, wup, wdn, layer, gpost, *, tm, tf):
    m = x.shape[0]
    d_ff = wup.shape[2]
    return pl.pallas_call(
        _mlp_kernel, grid=(m // tm, d_ff // tf),
        in_specs=[pl.BlockSpec((tm, D_MODEL), lambda i, f: (i, 0)),
                  pl.BlockSpec((1, D_MODEL), lambda i, f: (0, 0)),
                  pl.BlockSpec((None, D_MODEL, tf), lambda i, f: (layer, 0, f)),
                  pl.BlockSpec((None, tf, D_MODEL), lambda i, f: (layer, f, 0)),
                  pl.BlockSpec((1, D_MODEL), lambda i, f: (0, 0))],
        out_specs=pl.BlockSpec((tm, D_MODEL), lambda i, f: (i, 0)),
        out_shape=jax.ShapeDtypeStruct((m, D_MODEL), F32),
        scratch_shapes=[pltpu.VMEM((tm, D_MODEL), BF16), pltpu.VMEM((tm, D_MODEL), F32)],
        compiler_params=_params(("parallel", "arbitrary")), name="mlp")(
            x, gpre.reshape(1, D_MODEL), wup, wdn, gpost.reshape(1, D_MODEL))


def _mlp_cast_kernel(x_ref, gpre_ref, wup32_ref, wdn32_ref, gpost_ref, y_ref, wup16_ref, wdn16_ref,
                     xn_sc, acc_sc):
    wup16_ref[...] = wup32_ref[...].astype(BF16)
    wdn16_ref[...] = wdn32_ref[...].astype(BF16)
    _mlp_kernel(x_ref, gpre_ref, wup16_ref, wdn16_ref, gpost_ref, y_ref, xn_sc, acc_sc)


def _mlp_cast(x, gpre, wup32, wdn32, layer, gpost, *, tf):
    m = x.shape[0]
    d_ff = wup32.shape[2]
    return pl.pallas_call(
        _mlp_cast_kernel, grid=(1, d_ff // tf),
        in_specs=[pl.BlockSpec((m, D_MODEL), lambda i, f: (0, 0)),
                  pl.BlockSpec((1, D_MODEL), lambda i, f: (0, 0)),
                  pl.BlockSpec((None, D_MODEL, tf), lambda i, f: (layer, 0, f)),
                  pl.BlockSpec((None, tf, D_MODEL), lambda i, f: (layer, f, 0)),
                  pl.BlockSpec((1, D_MODEL), lambda i, f: (0, 0))],
        out_specs=(pl.BlockSpec((m, D_MODEL), lambda i, f: (0, 0)),
                   pl.BlockSpec((None, D_MODEL, tf), lambda i, f: (0, 0, f)),
                   pl.BlockSpec((None, tf, D_MODEL), lambda i, f: (0, f, 0))),
        out_shape=(jax.ShapeDtypeStruct((m, D_MODEL), F32),
                   jax.ShapeDtypeStruct((1, D_MODEL, d_ff), BF16),
                   jax.ShapeDtypeStruct((1, d_ff, D_MODEL), BF16)),
        scratch_shapes=[pltpu.VMEM((m, D_MODEL), BF16), pltpu.VMEM((m, D_MODEL), F32)],
        compiler_params=_params(("arbitrary", "arbitrary")), name="mlp_cast")(
            x, gpre.reshape(1, D_MODEL), wup32, wdn32, gpost.reshape(1, D_MODEL))


def _diff_lambda(lam_ref, lam_init):
    d1 = jnp.sum(lam_ref[0:1, :] * lam_ref[1:2, :], axis=-1, keepdims=True)
    d2 = jnp.sum(lam_ref[2:3, :] * lam_ref[3:4, :], axis=-1, keepdims=True)
    return jnp.exp(d1) - jnp.exp(d2) + lam_init


def _causal_steps(nq, newest_first):
    pairs = [(i, j) for i in range(nq) for j in (range(i, -1, -1) if newest_first else range(i + 1))]
    qt, kt = zip(*pairs)
    return jnp.asarray(qt, jnp.int32), jnp.asarray(kt, jnp.int32)


def _attn_a_kernel(qt_ref, kt_ref, lam_ref, gsub_ref, q_ref, k_ref, vt_ref, o_ref, m_sc, l_sc, acc_sc,
                   *, lam_init):
    qi = qt_ref[pl.program_id(2)]
    ki = kt_ref[pl.program_id(2)]

    @pl.when(ki == 0)
    def _():
        m_sc[...] = jnp.full_like(m_sc, NEG_INF)
        l_sc[...] = jnp.zeros_like(l_sc)
        acc_sc[...] = jnp.zeros_like(acc_sc)

    def step(diag):
        vt = vt_ref[...]
        tq, tk = q_ref.shape[0], k_ref.shape[0]
        if diag:
            keep = (lax.broadcasted_iota(jnp.int32, (tk, tq), 0)
                    <= lax.broadcasted_iota(jnp.int32, (tk, tq), 1))
        def logits(idx):
            c = idx % 2
            return _dot_nt(k_ref[:, c * HD_A:(c + 1) * HD_A], q_ref[:, idx * HD_A:(idx + 1) * HD_A])

        queue = [logits(idx) for idx in range(LOGITS_AHEAD)]
        for idx in range(2 * G_A):
            s = queue.pop(0)
            if idx + LOGITS_AHEAD < 2 * G_A:
                queue.append(logits(idx + LOGITS_AHEAD))
            if diag:
                s = jnp.where(keep, s, NEG_INF)
            m_prev = m_sc[idx]
            m_new = jnp.maximum(m_prev, jnp.max(s, axis=0, keepdims=True))
            alpha = jnp.exp2(m_prev - m_new)
            p = jnp.exp2(s - m_new)
            l_sc[idx] = alpha * l_sc[idx] + jnp.sum(p, axis=0, keepdims=True)
            acc_sc[idx] = alpha * acc_sc[idx] + _dot(vt, p.astype(BF16))
            m_sc[idx] = m_new

    @pl.when(ki < qi)
    def _():
        step(False)

    @pl.when(ki == qi)
    def _():
        step(True)
        lam = _diff_lambda(lam_ref, lam_init)
        for g in range(G_A):
            ot = acc_sc[2 * g] / l_sc[2 * g] - lam * (acc_sc[2 * g + 1] / l_sc[2 * g + 1])
            o = _rms(ot.T, gsub_ref[...]) * (1.0 - lam_init)
            o_ref[:, g * VD_A:(g + 1) * VD_A] = o.astype(o_ref.dtype)


def _attn_a_prompt(q, k, vt, lamvec, gsub, *, batch, seq, lam_init, tq):
    nq = seq // tq
    gw = G_A * VD_A
    qt, kt = _causal_steps(nq, newest_first=False)
    gs = pltpu.PrefetchScalarGridSpec(
        num_scalar_prefetch=2, grid=(batch, KV_A, qt.shape[0]),
        in_specs=[pl.BlockSpec((4, HD_A), lambda b, h, t, qt, kt: (0, 0)),
                  pl.BlockSpec((1, VD_A), lambda b, h, t, qt, kt: (0, 0)),
                  pl.BlockSpec((tq, gw), lambda b, h, t, qt, kt: (b * nq + qt[t], h)),
                  pl.BlockSpec((tq, 2 * HD_A), lambda b, h, t, qt, kt: (b * nq + kt[t], h)),
                  pl.BlockSpec((VD_A, tq), lambda b, h, t, qt, kt: (h, b * nq + kt[t]))],
        out_specs=pl.BlockSpec((tq, gw), lambda b, h, t, qt, kt: (b * nq + qt[t], h)),
        scratch_shapes=[pltpu.VMEM((2 * G_A, 1, tq), F32), pltpu.VMEM((2 * G_A, 1, tq), F32),
                        pltpu.VMEM((2 * G_A, VD_A, tq), F32)])
    return pl.pallas_call(
        functools.partial(_attn_a_kernel, lam_init=lam_init), grid_spec=gs,
        out_shape=jax.ShapeDtypeStruct((batch * seq, H_A * VD_A), BF16),
        compiler_params=_params(("parallel", "parallel", "arbitrary")),
        name="diff_attn_prompt")(qt, kt, lamvec, gsub.reshape(1, VD_A), q, k, vt)


def _page_rows(ref, offset):
    tokens = ref.shape[0]
    stride = math.prod(ref.shape[1:-1])
    flat = ref.reshape(tokens * stride, ref.shape[-1])
    return flat[pl.ds(offset, tokens, stride=stride), :]


def _attn_a_sample_kernel(pt_ref, lam_ref, gsub_ref, qz_ref, kn_ref, vn_ref, *rest, pps, lam_init):
    k_refs, v_refs = rest[:pps], rest[pps:2 * pps]
    o_ref, m_sc, l_sc, acc_sc = rest[2 * pps:]
    st = pl.program_id(1)

    @pl.when(st == 0)
    def _():
        for kv in range(KV_A):
            kn = kn_ref[0, :, kv * 2 * HD_A:(kv + 1) * 2 * HD_A].astype(BF16).astype(F32)
            vn = vn_ref[0, :, kv * VD_A:(kv + 1) * VD_A].astype(BF16).astype(F32)
            s = jnp.sum(qz_ref[0, kv].astype(F32) * kn, axis=-1, keepdims=True)
            m_sc[kv] = s
            l_sc[kv] = jnp.ones_like(s)
            acc_sc[kv] = jnp.broadcast_to(vn, acc_sc.shape[1:])

    for kv in range(KV_A):
        k_kv = jnp.concatenate(
            [jnp.concatenate([_page_rows(r, 2 * kv + c).astype(BF16) for c in range(2)], axis=1)
             for r in k_refs], axis=0)
        v_kv = jnp.concatenate(
            [jnp.concatenate([_page_rows(r, h * KV_A + kv).astype(BF16) for h in range(VD_A // LANES)], axis=1)
             for r in v_refs], axis=0)
        s = _dot_nt(qz_ref[0, kv], k_kv)
        m_prev = m_sc[kv]
        m_new = jnp.maximum(m_prev, jnp.max(s, axis=-1, keepdims=True))
        alpha = jnp.exp2(m_prev - m_new)
        p = jnp.exp2(s - m_new)
        l_sc[kv] = alpha * l_sc[kv] + jnp.sum(p, axis=-1, keepdims=True)
        acc_sc[kv] = alpha * acc_sc[kv] + _dot(p.astype(BF16), v_kv)
        m_sc[kv] = m_new

    @pl.when(st == pl.num_programs(1) - 1)
    def _():
        lam = _diff_lambda(lam_ref, lam_init)
        for kv in range(KV_A):
            on = acc_sc[kv] / l_sc[kv]
            o = on[0:G_A] - lam * on[G_A:2 * G_A]
            o_ref[0, kv] = _rms(o, gsub_ref[...]) * (1.0 - lam_init)


def _attn_a_sample(pt, lamvec, gsub, qz, kn, vn, cache_k, cache_v, *, layer, lam_init, pps):
    db = qz.shape[0]
    n_pages = pt.shape[0] // db
    page = cache_k.shape[2]

    def page_map(i, tail):
        return lambda b, s, pt_ref: (layer, pt_ref[b * n_pages + s * pps + i]) + tail

    in_specs = [pl.BlockSpec((4, HD_A), lambda b, s, pt_ref: (0, 0)),
                pl.BlockSpec((1, VD_A), lambda b, s, pt_ref: (0, 0)),
                pl.BlockSpec((1, KV_A, 2 * G_A, 2 * HD_A), lambda b, s, pt_ref: (b, 0, 0, 0)),
                pl.BlockSpec((1, 1, KV_A * 2 * HD_A), lambda b, s, pt_ref: (b, 0, 0)),
                pl.BlockSpec((1, 1, KV_A * VD_A), lambda b, s, pt_ref: (b, 0, 0))]
    in_specs += [pl.BlockSpec((None, None, page, KV_A, 2, HD_A), page_map(i, (0, 0, 0, 0)))
                 for i in range(pps)]
    in_specs += [pl.BlockSpec((None, None, page, VD_A // LANES, KV_A, LANES), page_map(i, (0, 0, 0, 0)))
                 for i in range(pps)]
    gs = pltpu.PrefetchScalarGridSpec(
        num_scalar_prefetch=1, grid=(db, n_pages // pps), in_specs=in_specs,
        out_specs=pl.BlockSpec((1, KV_A, G_A, VD_A), lambda b, s, pt_ref: (b, 0, 0, 0)),
        scratch_shapes=[pltpu.VMEM((KV_A, 2 * G_A, 1), F32), pltpu.VMEM((KV_A, 2 * G_A, 1), F32),
                        pltpu.VMEM((KV_A, 2 * G_A, VD_A), F32)])
    return pl.pallas_call(
        functools.partial(_attn_a_sample_kernel, pps=pps, lam_init=lam_init),
        grid_spec=gs, out_shape=jax.ShapeDtypeStruct((db, KV_A, G_A, VD_A), F32),
        compiler_params=_params(("parallel", "arbitrary")), name="diff_attn_sample")(
            pt, lamvec, gsub.reshape(1, VD_A), qz, kn, vn,
            *([cache_k] * pps), *([cache_v] * pps))


def _attn_b_kernel(sink_ref, q_ref, kc_ref, kp_ref, vc_ref, vp_ref, o_ref):
    i = pl.program_id(1)
    w = q_ref.shape[0]
    scale = HD_B ** -0.5
    kk = jnp.concatenate([kp_ref[...], kc_ref[...]], axis=0)
    vv = jnp.concatenate([vp_ref[...], vc_ref[...]], axis=0)
    row = lax.broadcasted_iota(jnp.int32, (w, 2 * w), 0)
    col = lax.broadcasted_iota(jnp.int32, (w, 2 * w), 1)
    keep = (col >= row) & (col <= row + WINDOW) & ((i > 0) | (col >= w))
    for kv in range(KV_B):
        k = kk[:, kv * HD_B:(kv + 1) * HD_B]
        v = vv[:, kv * HD_B:(kv + 1) * HD_B]
        outs = []
        for g in range(G_B):
            h = kv * G_B + g
            q = q_ref[:, h * HD_B:(h + 1) * HD_B]
            s = jnp.where(keep, _dot_nt(q, k) * scale, NEG_INF)
            sink = sink_ref[h]
            mx = jnp.maximum(jnp.max(s, axis=-1, keepdims=True), sink)
            e = jnp.exp(s - mx)
            denom = jnp.sum(e, axis=-1, keepdims=True) + jnp.exp(sink - mx)
            outs.append(_dot(e.astype(BF16), v) / denom)
        for t in range(G_B // 2):
            h0 = kv * G_B + 2 * t
            o_ref[:, h0 * HD_B:(h0 + 2) * HD_B] = jnp.concatenate(
                [outs[2 * t], outs[2 * t + 1]], axis=1).astype(o_ref.dtype)


def _attn_b_prompt(sink, q, k, v, *, batch, seq):
    nb = seq // WINDOW
    kvw = KV_B * HD_B
    cur = lambda b, i: (b * nb + i, 0)
    prev = lambda b, i: (b * nb + jnp.maximum(i - 1, 0), 0)
    return pl.pallas_call(
        _attn_b_kernel, grid=(batch, nb),
        in_specs=[pl.BlockSpec(memory_space=pltpu.SMEM),
                  pl.BlockSpec((WINDOW, D_MODEL), cur),
                  pl.BlockSpec((WINDOW, kvw), cur), pl.BlockSpec((WINDOW, kvw), prev),
                  pl.BlockSpec((WINDOW, kvw), cur), pl.BlockSpec((WINDOW, kvw), prev)],
        out_specs=pl.BlockSpec((WINDOW, D_MODEL), cur),
        out_shape=jax.ShapeDtypeStruct((batch * seq, D_MODEL), BF16),
        compiler_params=_params(("parallel", "parallel")), name="swa_prompt")(sink, q, k, k, v, v)


def _attn_b_sample_kernel(sink_ref, qz_ref, kn_ref, vn_ref, bk_ref, bv_ref, o_ref, nk_ref, nv_ref,
                          *, past_len):
    wb = bk_ref.shape[1]
    scale = HD_B ** -0.5
    bk = bk_ref[0]
    bv = bv_ref[0]
    kn = kn_ref[0]
    vn = vn_ref[0]
    nk_ref[0, 0:wb - 1, :] = bk[1:wb, :]
    nk_ref[0, wb - 1:wb, :] = kn
    nv_ref[0, 0:wb - 1, :] = bv[1:wb, :]
    nv_ref[0, wb - 1:wb, :] = vn
    dist = wb - lax.broadcasted_iota(jnp.int32, (G_B, wb), 1)
    keep = (dist >= 0) & (dist <= WINDOW) & (past_len - dist >= 0)
    bk16 = bk.astype(BF16)
    bv16 = bv.astype(BF16)
    kn16 = kn.astype(BF16).astype(F32)
    vn16 = vn.astype(BF16).astype(F32)
    for kv in range(KV_B):
        qz = qz_ref[0, kv]
        s = jnp.where(keep, _dot_nt(qz, bk16) * scale, NEG_INF)
        s_new = jnp.sum(qz.astype(F32) * kn16, axis=-1, keepdims=True) * scale
        sink = sink_ref[kv]
        mx = jnp.maximum(jnp.maximum(jnp.max(s, axis=-1, keepdims=True), s_new), sink)
        e = jnp.exp(s - mx)
        e_new = jnp.exp(s_new - mx)
        denom = jnp.sum(e, axis=-1, keepdims=True) + e_new + jnp.exp(sink - mx)
        o = _dot(e.astype(BF16), bv16) + e_new.astype(BF16).astype(F32) * vn16
        o_ref[0, kv] = (o / denom)[:, kv * HD_B:(kv + 1) * HD_B]


def _attn_b_sample(sink, qz, kn, vn, buf_k, buf_v, *, past_len):
    db, wb, kvw = buf_k.shape
    row = lambda b: (b, 0, 0)
    return pl.pallas_call(
        functools.partial(_attn_b_sample_kernel, past_len=past_len), grid=(db,),
        in_specs=[pl.BlockSpec((KV_B, G_B, 1), lambda b: (0, 0, 0)),
                  pl.BlockSpec((1, KV_B, G_B, kvw), lambda b: (b, 0, 0, 0)),
                  pl.BlockSpec((1, 1, kvw), row), pl.BlockSpec((1, 1, kvw), row),
                  pl.BlockSpec((1, wb, kvw), row), pl.BlockSpec((1, wb, kvw), row)],
        out_specs=(pl.BlockSpec((1, KV_B, G_B, HD_B), lambda b: (b, 0, 0, 0)),
                   pl.BlockSpec((1, wb, kvw), row), pl.BlockSpec((1, wb, kvw), row)),
        out_shape=(jax.ShapeDtypeStruct((db, KV_B, G_B, HD_B), F32),
                   jax.ShapeDtypeStruct((db, wb, kvw), F32), jax.ShapeDtypeStruct((db, wb, kvw), F32)),
        compiler_params=_params(("parallel",)), name="swa_sample")(sink, qz, kn, vn, buf_k, buf_v)


def _suffix_matrix():
    j = np.arange(LANES)[:, None]
    s = np.arange(LANES)[None, :]
    u = np.concatenate([(j > s).astype(np.float32), np.ones((LANES, LANES), np.float32)], axis=1)
    return jnp.asarray(np.concatenate([u, u], axis=0), dtype=BF16)


def _suffix_sums(w, u):
    hi = w.astype(BF16)
    lo = (w - hi.astype(F32)).astype(BF16)
    return _dot(jnp.concatenate([hi, lo], axis=1), u)


def _attn_c_kernel(qt_ref, kt_ref, u_ref, q_ref, k_ref, v_ref, o_ref, carry_sc, acc_sc):
    qi = qt_ref[pl.program_id(2)]
    kb = kt_ref[pl.program_id(2)]

    @pl.when(kb == qi)
    def _():
        carry_sc[...] = jnp.zeros_like(carry_sc)
        acc_sc[...] = jnp.zeros_like(acc_sc)

    def step(diag):
        k = k_ref[...]
        v = v_ref[...]
        u = u_ref[...]
        tq, tk = q_ref.shape[0], k.shape[0]
        if diag:
            keep = (lax.broadcasted_iota(jnp.int32, (tq, tk), 1)
                    < lax.broadcasted_iota(jnp.int32, (tq, tk), 0))
        def logits(g):
            return _dot_nt(q_ref[:, g * HD_C:(g + 1) * HD_C], k)

        z_next = logits(0)
        for g in range(G_C):
            z = z_next
            if g + 1 < G_C:
                z_next = logits(g + 1)
            lb, w = _stick_logs(z)
            if diag:
                w = jnp.where(keep, w, 0.0)
            carry = carry_sc[g]
            a_blocks = [None] * (tk // LANES)
            for sb in reversed(range(tk // LANES)):
                cols = slice(sb * LANES, (sb + 1) * LANES)
                sums = _suffix_sums(w[:, cols], u)
                a = jnp.exp2(lb[:, cols] - (carry + sums[:, :LANES]))
                if diag:
                    a = jnp.where(keep[:, cols], a, 0.0)
                a_blocks[sb] = a.astype(BF16)
                carry = carry + sums[:, LANES:]
            carry_sc[g] = carry
            acc_sc[g] += _dot(jnp.concatenate(a_blocks, axis=1), v)

    @pl.when(kb == qi)
    def _():
        step(True)

    @pl.when(kb < qi)
    def _():
        step(False)

    @pl.when(kb == 0)
    def _():
        for g in range(G_C):
            o_ref[:, g * HD_C:(g + 1) * HD_C] = acc_sc[g].astype(o_ref.dtype)


def _attn_c_prompt(u, q, k, v, *, batch, seq, tq):
    nq = seq // tq
    gw = G_C * HD_C
    qt, kt = _causal_steps(nq, newest_first=True)
    kmap = lambda b, h, t, qt, kt: (b * nq + kt[t], h)
    gs = pltpu.PrefetchScalarGridSpec(
        num_scalar_prefetch=2, grid=(batch, KV_C, qt.shape[0]),
        in_specs=[pl.BlockSpec((2 * LANES, 2 * LANES), lambda b, h, t, qt, kt: (0, 0)),
                  pl.BlockSpec((tq, gw), lambda b, h, t, qt, kt: (b * nq + qt[t], h)),
                  pl.BlockSpec((tq, HD_C), kmap), pl.BlockSpec((tq, HD_C), kmap)],
        out_specs=pl.BlockSpec((tq, gw), lambda b, h, t, qt, kt: (b * nq + qt[t], h)),
        scratch_shapes=[pltpu.VMEM((G_C, tq, LANES), F32), pltpu.VMEM((G_C, tq, HD_C), F32)])
    return pl.pallas_call(
        _attn_c_kernel, grid_spec=gs,
        out_shape=jax.ShapeDtypeStruct((batch * seq, H_C * HD_C), BF16),
        compiler_params=_params(("parallel", "parallel", "arbitrary")),
        name="sb_attn_prompt")(qt, kt, u, q, k, v)


def _attn_c_sample_kernel(pt_ref, u_ref, qz_ref, *rest, pps):
    k_refs, v_refs = rest[:pps], rest[pps:2 * pps]
    o_ref, carry_sc, acc_sc = rest[2 * pps:]
    st = pl.program_id(1)

    @pl.when(st == 0)
    def _():
        carry_sc[...] = jnp.zeros_like(carry_sc)
        acc_sc[...] = jnp.zeros_like(acc_sc)

    page = k_refs[0].shape[0]
    z = None
    for kv in range(KV_C):
        k_kv = jnp.concatenate([_page_rows(r, kv).astype(BF16) for r in k_refs], axis=0)
        zk = _dot_nt(qz_ref[0, kv], k_kv)
        z = zk if z is None else z + zk
    lb, w = _stick_logs(z)
    w_rows = jnp.concatenate([w[:, i * page:(i + 1) * page] for i in range(pps)], axis=0)
    sums = _suffix_sums(w_rows, u_ref[...])
    carry = carry_sc[...]
    a_blocks = [None] * pps
    for i in reversed(range(pps)):
        rows = slice(i * H_C, (i + 1) * H_C)
        a_blocks[i] = jnp.exp2(lb[:, i * page:(i + 1) * page] - (carry + sums[rows, :LANES]))
        carry = carry + sums[rows, LANES:]
    carry_sc[...] = carry
    a = jnp.concatenate(a_blocks, axis=1)
    row_kv = lax.broadcasted_iota(jnp.int32, a.shape, 0) // G_C
    acc = acc_sc[...]
    for kv in range(KV_C):
        v_kv = jnp.concatenate([_page_rows(r, kv).astype(BF16) for r in v_refs], axis=0)
        acc = acc + _dot(jnp.where(row_kv == kv, a, 0.0).astype(BF16), v_kv)
    acc_sc[...] = acc

    @pl.when(st == pl.num_programs(1) - 1)
    def _():
        o_ref[0] = acc


def _attn_c_sample(pt, u, qz, cache_k, cache_v, *, layer, pps):
    db = qz.shape[0]
    n_pages = pt.shape[0] // db
    n_steps = n_pages // pps
    page = cache_k.shape[2]

    def page_map(i):
        return lambda b, s, pt_ref: (layer, pt_ref[b * n_pages + (n_steps - 1 - s) * pps + i], 0, 0, 0)

    in_specs = [pl.BlockSpec((2 * LANES, 2 * LANES), lambda b, s, pt_ref: (0, 0)),
                pl.BlockSpec((1, KV_C, H_C, HD_C), lambda b, s, pt_ref: (b, 0, 0, 0))]
    in_specs += [pl.BlockSpec((None, None, page, KV_C, HD_C), page_map(i)) for i in range(pps)] * 2
    gs = pltpu.PrefetchScalarGridSpec(
        num_scalar_prefetch=1, grid=(db, n_steps), in_specs=in_specs,
        out_specs=pl.BlockSpec((1, H_C, HD_C), lambda b, s, pt_ref: (b, 0, 0)),
        scratch_shapes=[pltpu.VMEM((H_C, LANES), F32), pltpu.VMEM((H_C, HD_C), F32)])
    return pl.pallas_call(
        functools.partial(_attn_c_sample_kernel, pps=pps),
        grid_spec=gs, out_shape=jax.ShapeDtypeStruct((db, H_C, HD_C), F32),
        compiler_params=_params(("parallel", "arbitrary")), name="sb_attn_sample")(
            pt, u, qz, *([cache_k] * pps), *([cache_v] * pps))


def _pick(n, pref):
    return pref if n % pref == 0 else n


def kernel(x_prompt, x_sample, cache_a_k, cache_a_v, cache_c_k, cache_c_v, state_win_k, state_win_v, page_table, g_mix_pre, g_mix_post, g_ffn_pre, g_ffn_post, w_a_qkv, lam_q1, lam_k1, lam_q2, lam_k2, g_a_sub, w_a_out, w_b_qkv, b_sink, w_b_out, w_c_qkv, w_c_out, w_up, w_down):
    batch, seq, _ = x_prompt.shape
    db, dseq, _ = x_sample.shape
    assert dseq == 1
    depth = g_mix_pre.shape[0]
    n_pages = page_table.shape[1]
    page = cache_a_k.shape[2]
    past_len = n_pages * page
    m_p = batch * seq
    tm = _pick(m_p, 256)
    tq = _pick(seq, 512)
    tm_mlp = _pick(m_p, 512)
    pps = _pick(n_pages, 32)
    scale_a = HD_A ** -0.5 * math.log2(math.e)
    scale_c = HD_C ** -0.5 * math.log2(math.e)

    xp = x_prompt.reshape(m_p, D_MODEL)
    xs = x_sample.reshape(db, D_MODEL)
    pt = page_table.reshape(-1)
    pos_p = jnp.arange(seq, dtype=jnp.int32)
    pos_s = jnp.full((db,), past_len, dtype=jnp.int32)
    tabs = {hd: (_rope_tables(pos_p, hd), _rope_tables(pos_s, hd)) for hd in (HD_A, HD_B)}
    u = _suffix_matrix()
    cache_a_v6 = cache_a_v.reshape(*cache_a_v.shape[:3], KV_A, VD_A // LANES, LANES).transpose(0, 1, 2, 4, 3, 5)
    v_rows_a = tuple(half * KV_A + head for head in range(KV_A) for half in range(VD_A // LANES))

    w_qkv16 = (w_a_qkv.astype(BF16), w_b_qkv.astype(BF16), w_c_qkv.astype(BF16))
    w_out16 = (w_a_out.astype(BF16), w_b_out.astype(BF16), w_c_out.astype(BF16))

    def token_mixers(xp, xs, o, os_, w_out, j, i):
        xs = _outproj(os_, w_out, j, xs, g_mix_post[i], tm=db)
        xs, w_up16, w_down16 = _mlp_cast(xs, g_ffn_pre[i], w_up, w_down, i, g_ffn_post[i], tf=512)
        xp = _outproj(o, w_out, j, xp, g_mix_post[i], tm=tm)
        xp = _mlp(xp, g_ffn_pre[i], w_up16, w_down16, 0, g_ffn_post[i], tm=tm_mlp, tf=512)
        return xp, xs

    outs = {name: [] for name in ("a_kp", "a_vp", "a_ks", "a_vs", "b_kp", "b_vp", "b_ks", "b_vs",
                                  "c_kp", "c_vp", "c_ks", "c_vs")}
    for i in range(depth):
        kind, j = i % N_MIXERS, i // N_MIXERS
        w, w_out = w_qkv16[kind], w_out16[kind]
        if kind == 0:
            lam_init = 0.8 - 0.6 * math.exp(-0.3 * i)
            lamvec = jnp.stack([lam_q1[j], lam_k1[j], lam_q2[j], lam_k2[j]])
            kw, vw = KV_A * 2 * HD_A, KV_A * VD_A
            q, k32, v32, k16, v16t = _qkv(xp, g_mix_pre[i], w, j,tabs[HD_A][0], rope_kind=HD_A, kw=kw, vw=vw, tm=tm,
                                          q_scale=scale_a, v_transposed=True,
                                          k_rows=tuple(range(kw // LANES)), v_rows=v_rows_a)
            o = _attn_a_prompt(q, k16, v16t, lamvec, g_a_sub[j], batch=batch, seq=seq, lam_init=lam_init, tq=tq)
            qs, ks32, vs32, _, _ = _qkv(xs, g_mix_pre[i], w, j,tabs[HD_A][1], rope_kind=HD_A, kw=kw, vw=vw, tm=db,
                                        q_scale=scale_a)
            q5 = qs.reshape(db, KV_A, G_A, 2, HD_A)
            zero = jnp.zeros((db, KV_A, G_A, HD_A), BF16)
            qz = jnp.concatenate([jnp.concatenate([q5[:, :, :, 0], zero], axis=-1),
                                  jnp.concatenate([zero, q5[:, :, :, 1]], axis=-1)], axis=2)
            os_ = _attn_a_sample(pt, lamvec, g_a_sub[j], qz, ks32.reshape(db, 1, kw), vs32.reshape(db, 1, vw),
                                 cache_a_k, cache_a_v6, layer=j, lam_init=lam_init, pps=pps)
            os_ = os_.reshape(db, H_A * VD_A)
            outs["a_kp"].append(k32.reshape(batch, seq, KV_A, 2, HD_A))
            outs["a_vp"].append(v32.reshape(batch, seq, VD_A // LANES, KV_A, LANES)
                                .transpose(0, 1, 3, 2, 4).reshape(batch, seq, KV_A, VD_A))
            outs["a_ks"].append(ks32.reshape(db, 1, KV_A, 2, HD_A))
            outs["a_vs"].append(vs32.reshape(db, 1, KV_A, VD_A))
        elif kind == 1:
            kw = vw = KV_B * HD_B
            q, k32, v32, k16, v16 = _qkv(xp, g_mix_pre[i], w, j,tabs[HD_B][0], rope_kind=HD_B, kw=kw, vw=vw, tm=tm,
                                         q_scale=1.0)
            o = _attn_b_prompt(b_sink[j], q, k16, v16, batch=batch, seq=seq)
            qs, ks32, vs32, _, _ = _qkv(xs, g_mix_pre[i], w, j,tabs[HD_B][1], rope_kind=HD_B, kw=kw, vw=vw, tm=db,
                                        q_scale=1.0)
            q4 = qs.reshape(db, KV_B, G_B, HD_B)
            qz = jnp.stack([jnp.pad(q4[:, kv], ((0, 0), (0, 0), (kv * HD_B, (KV_B - 1 - kv) * HD_B)))
                            for kv in range(KV_B)], axis=1)
            wb = state_win_k.shape[2]
            os_, nk, nv = _attn_b_sample(b_sink[j].reshape(KV_B, G_B, 1), qz, ks32.reshape(db, 1, kw),
                                         vs32.reshape(db, 1, vw), state_win_k[j].reshape(db, wb, kw),
                                         state_win_v[j].reshape(db, wb, vw), past_len=past_len)
            os_ = os_.reshape(db, H_B * HD_B)
            outs["b_kp"].append(k32.reshape(batch, seq, KV_B, HD_B)[:, seq - WINDOW:])
            outs["b_vp"].append(v32.reshape(batch, seq, KV_B, HD_B)[:, seq - WINDOW:])
            outs["b_ks"].append(nk.reshape(db, wb, KV_B, HD_B))
            outs["b_vs"].append(nv.reshape(db, wb, KV_B, HD_B))
        else:
            kw = vw = KV_C * HD_C
            q, k32, v32, k16, v16 = _qkv(xp, g_mix_pre[i], w, j, None, rope_kind=None, kw=kw, vw=vw, tm=tm,
                                         q_scale=scale_c, k_rows=tuple(range(kw // LANES)),
                                         v_rows=tuple(range(vw // LANES)))
            o = _attn_c_prompt(u, q, k16, v16, batch=batch, seq=seq, tq=tq)
            qs, ks32, vs32, _, _ = _qkv(xs, g_mix_pre[i], w, j,None, rope_kind=None, kw=kw, vw=vw, tm=db,
                                        q_scale=scale_c)
            q4 = qs.reshape(db, KV_C, G_C, HD_C)
            qz = jnp.stack([jnp.pad(q4[:, kv], ((0, 0), (kv * G_C, (KV_C - 1 - kv) * G_C), (0, 0)))
                            for kv in range(KV_C)], axis=1)
            os_ = _attn_c_sample(pt, u, qz, cache_c_k, cache_c_v, layer=j, pps=pps).reshape(db, H_C * HD_C)
            outs["c_kp"].append(k32.reshape(batch, seq, KV_C, HD_C))
            outs["c_vp"].append(v32.reshape(batch, seq, KV_C, HD_C))
            outs["c_ks"].append(ks32.reshape(db, 1, KV_C, HD_C))
            outs["c_vs"].append(vs32.reshape(db, 1, KV_C, HD_C))
        xp, xs = token_mixers(xp, xs, o, os_, w_out, j, i)
    state = [jnp.stack(outs[name]) for name in ("a_kp", "a_vp", "a_ks", "a_vs", "b_kp", "b_vp", "b_ks", "b_vs",
                                               "c_kp", "c_vp", "c_ks", "c_vs")]
    return (xp.reshape(batch, seq, D_MODEL), xs.reshape(db, 1, D_MODEL), *state)
```
